```python
import math
import jax, jax.numpy as jnp
from jax import lax
import numpy as np

D_MODEL = 1024
BATCH = 16
SEQ = 2048
DEPTH = 1
DEC_BATCH = 128
DEC_SEQ = 8
PAST_LEN = 8192
PAGE_SIZE = 128

H_A = 4
DK_A = 64
DV_A = 2 * DK_A
H_B = 8
QK_NOPE = 64
QK_ROPE = 32
V_HEAD = 64
Q_LORA = 384
KV_LORA = 256
D_MIX = H_A * DV_A + H_B * V_HEAD
OFF_K = H_A * 2 * DK_A
OFF_V = 2 * H_A * 2 * DK_A
OFF_CQ = OFF_V + H_A * DV_A
OFF_CKV = OFF_CQ + Q_LORA
OFF_KR = OFF_CKV + KV_LORA
D_IN = OFF_KR + QK_ROPE
D_FF = 4 * D_MODEL
ROPE_THETA = 10000.0
Q_BLOCK = 128
EPS = 1e-6
NEG_INF = -1e30

kernel_name = "hymba_diffattn_mla_adaln_step"


def rms_norm(x):
    xf = x.astype(jnp.float32)
    return (xf * lax.rsqrt(jnp.mean(xf * xf, axis=-1, keepdims=True) + EPS)).astype(x.dtype)


def rope(x, pos):
    d = x.shape[-1]
    inv = 1.0 / (ROPE_THETA ** (jnp.arange(0, d, 2, dtype=jnp.float32) / d))
    ang = pos.astype(jnp.float32)[:, None] * inv[None, :]
    shp = (1, pos.shape[0]) + (1,) * (x.ndim - 3) + (d // 2,)
    cos = jnp.cos(ang).reshape(shp)
    sin = jnp.sin(ang).reshape(shp)
    xf = x.astype(jnp.float32)
    x1, x2 = xf[..., : d // 2], xf[..., d // 2:]
    return jnp.concatenate([x1 * cos - x2 * sin, x2 * cos + x1 * sin], axis=-1).astype(x.dtype)


def adaln_mod(c, w_ada, b_ada):
    m = jax.nn.silu(c) @ w_ada + b_ada
    return [t[:, None, :] for t in jnp.split(m, 6, axis=-1)]


def modulate(x, shift, scale):
    return rms_norm(x) * (1.0 + scale) + shift


def diff_lambda(lq1, lk1, lq2, lk2, lam_init):
    f = jnp.float32
    return (jnp.exp(jnp.sum(lq1.astype(f) * lk1.astype(f)))
            - jnp.exp(jnp.sum(lq2.astype(f) * lk2.astype(f))) + lam_init)


def mixer_inputs(h, pos, w_in, g_q, w_uq, w_uk, g_kv):
    B, T, _ = h.shape
    p = h @ w_in
    q_a = rope(p[..., :OFF_K].reshape(B, T, H_A, 2, DK_A), pos)
    k_a = rope(p[..., OFF_K:OFF_V].reshape(B, T, H_A, 2, DK_A), pos)
    v_a = p[..., OFF_V:OFF_CQ].reshape(B, T, H_A, DV_A)
    c_q = rms_norm(p[..., OFF_CQ:OFF_CKV]) * g_q
    q_b = (c_q @ w_uq).reshape(B, T, H_B, QK_NOPE + QK_ROPE)
    q_lat = jnp.einsum('bthn,hrn->bthr', q_b[..., :QK_NOPE], w_uk)
    q_rope = rope(q_b[..., QK_NOPE:], pos)
    ckv = rms_norm(p[..., OFF_CKV:OFF_KR]) * g_kv
    k_rope = rope(p[..., OFF_KR:], pos)
    return q_a, k_a, v_a, q_lat, q_rope, ckv, k_rope


def diff_attend(q, k, v, lam, mask):
    s = jnp.einsum('bqhmd,bkhmd->bhmqk', q, k).astype(jnp.float32) * (DK_A ** -0.5)
    p = jax.nn.softmax(jnp.where(mask, s, NEG_INF), axis=-1)
    a = p[:, :, 0] - lam * p[:, :, 1]
    return jnp.einsum('bhqk,bkhd->bqhd', a.astype(v.dtype), v)


def mla_attend(q_lat, q_rope, ckv, k_rope, mask):
    s = (jnp.einsum('bqhr,bkr->bhqk', q_lat, ckv)
         + jnp.einsum('bqhd,bkd->bhqk', q_rope, k_rope)).astype(jnp.float32)
    s = s * ((QK_NOPE + QK_ROPE) ** -0.5)
    p = jax.nn.softmax(jnp.where(mask, s, NEG_INF), axis=-1)
    return jnp.einsum('bhqk,bkr->bqhr', p.astype(ckv.dtype), ckv)


def prompt_attention(q_a, k_a, v_a, q_lat, q_rope, ckv, k_rope, lam):
    T = q_a.shape[1]
    tri = jnp.tril(jnp.ones((Q_BLOCK, Q_BLOCK), dtype=bool))
    outs_a, outs_b = [], []
    for i in range(T // Q_BLOCK):
        q0, q1 = i * Q_BLOCK, (i + 1) * Q_BLOCK
        mask = jnp.concatenate([jnp.ones((Q_BLOCK, q0), dtype=bool), tri], axis=1)
        outs_a.append(diff_attend(q_a[:, q0:q1], k_a[:, :q1], v_a[:, :q1], lam, mask))
        outs_b.append(mla_attend(q_lat[:, q0:q1], q_rope[:, q0:q1], ckv[:, :q1], k_rope[:, :q1], mask))
    return jnp.concatenate(outs_a, axis=1), jnp.concatenate(outs_b, axis=1)


def sample_attention(q_a, k_a, v_a, q_lat, q_rope, ckv, k_rope, lam,
                     page_table, cache_k, cache_v, cache_ckv, cache_kr, layer):
    T = q_a.shape[1]
    n_past = page_table.shape[1] * PAGE_SIZE
    mask = jnp.concatenate([jnp.ones((T, n_past), dtype=bool),
                            jnp.tril(jnp.ones((T, T), dtype=bool))], axis=1)

    def one_seq(args):
        qa, ka, va, ql, qr, cn, kr, pages = args
        def past(cache):
            return cache[layer, pages].reshape((n_past,) + cache.shape[3:])
        ka_all = jnp.concatenate([past(cache_k), ka], axis=0)
        va_all = jnp.concatenate([past(cache_v), va], axis=0)
        cn_all = jnp.concatenate([past(cache_ckv), cn], axis=0)
        kr_all = jnp.concatenate([past(cache_kr), kr], axis=0)
        oa = diff_attend(qa[None], ka_all[None], va_all[None], lam, mask)[0]
        ob = mla_attend(ql[None], qr[None], cn_all[None], kr_all[None], mask)[0]
        return oa, ob

    return lax.map(one_seq, (q_a, k_a, v_a, q_lat, q_rope, ckv, k_rope, page_table))


def mixer_output(o_a, o_lat, g_sub, w_uv, w_out, lam_init):
    B, T = o_a.shape[:2]
    o_a = (rms_norm(o_a) * g_sub * (1.0 - lam_init)).reshape(B, T, H_A * DV_A)
    o_b = jnp.einsum('bthr,hrv->bthv', o_lat, w_uv).reshape(B, T, H_B * V_HEAD)
    return jnp.concatenate([o_a, o_b], axis=-1) @ w_out


def decoder_layer(x, c, pos, attend, w_ada, b_ada, w_in, g_q, w_uq, w_uk, g_kv,
                  g_sub, w_uv, w_out, w_ff1, w_ff2, lam, lam_init):
    sh1, sc1, gt1, sh2, sc2, gt2 = adaln_mod(c, w_ada, b_ada)
    q_a, k_a, v_a, q_lat, q_rope, ckv, k_rope = mixer_inputs(
        modulate(x, sh1, sc1), pos, w_in, g_q, w_uq, w_uk, g_kv)
    o_a, o_lat = attend(q_a, k_a, v_a, q_lat, q_rope, ckv, k_rope, lam)
    x = x + gt1 * mixer_output(o_a, o_lat, g_sub, w_uv, w_out, lam_init)
    hm = modulate(x, sh2, sc2)
    x = x + gt2 * (jnp.square(jax.nn.relu(hm @ w_ff1)) @ w_ff2)
    return x, (k_a, v_a, ckv, k_rope)


def setup_inputs(seed: int = 0) -> dict:
    key = jax.random.key(seed)
    ks = jax.random.split(key, 32)
    f = jnp.float32
    def nrm(k, shape, s=1.0):
        return jax.random.normal(k, shape, f) * s
    n_pages = PAST_LEN // PAGE_SIZE
    n_used = DEC_BATCH * n_pages
    n_phys = n_used + max(1, n_used // 4)
    page_table = jax.random.permutation(ks[8], n_phys)[:n_used].reshape(DEC_BATCH, n_pages).astype(jnp.int32)
    return {
        "x_prompt": nrm(ks[0], (BATCH, SEQ, D_MODEL)),
        "x_sample": nrm(ks[1], (DEC_BATCH, DEC_SEQ, D_MODEL)),
        "c_prompt": nrm(ks[2], (BATCH, D_MODEL)),
        "c_sample": nrm(ks[3], (DEC_BATCH, D_MODEL)),
        "cache_diff_k": nrm(ks[4], (DEPTH, n_phys, PAGE_SIZE, H_A, 2, DK_A)),
        "cache_diff_v": nrm(ks[5], (DEPTH, n_phys, PAGE_SIZE, H_A, DV_A)),
        "cache_mla_ckv": nrm(ks[6], (DEPTH, n_phys, PAGE_SIZE, KV_LORA)),
        "cache_mla_krope": nrm(ks[7], (DEPTH, n_phys, PAGE_SIZE, QK_ROPE)),
        "page_table": page_table,
        "w_ada": nrm(ks[9], (DEPTH, D_MODEL, 6 * D_MODEL), 0.5 * D_MODEL ** -0.5),
        "b_ada": nrm(ks[10], (DEPTH, 6 * D_MODEL), 0.01),
        "w_in": nrm(ks[11], (DEPTH, D_MODEL, D_IN), D_MODEL ** -0.5),
        "g_q": 1.0 + nrm(ks[12], (DEPTH, Q_LORA), 0.02),
        "w_uq": nrm(ks[13], (DEPTH, Q_LORA, H_B * (QK_NOPE + QK_ROPE)), Q_LORA ** -0.5),
        "w_uk": nrm(ks[14], (DEPTH, H_B, KV_LORA, QK_NOPE), KV_LORA ** -0.5),
        "g_kv": 1.0 + nrm(ks[15], (DEPTH, KV_LORA), 0.02),
        "lambda_q1": nrm(ks[16], (DEPTH, DK_A), 0.1),
        "lambda_k1": nrm(ks[17], (DEPTH, DK_A), 0.1),
        "lambda_q2": nrm(ks[18], (DEPTH, DK_A), 0.1),
        "lambda_k2": nrm(ks[19], (DEPTH, DK_A), 0.1),
        "g_sub": 1.0 + nrm(ks[20], (DEPTH, DV_A), 0.02),
        "w_uv": nrm(ks[21], (DEPTH, H_B, KV_LORA, V_HEAD), KV_LORA ** -0.5),
        "w_out": nrm(ks[22], (DEPTH, D_MIX, D_MODEL), D_MIX ** -0.5),
        "w_ff1": nrm(ks[23], (DEPTH, D_MODEL, D_FF), D_MODEL ** -0.5),
        "w_ff2": nrm(ks[24], (DEPTH, D_FF, D_MODEL), 0.5 * D_FF ** -0.5),
        "g_final": 1.0 + nrm(ks[25], (D_MODEL,), 0.02),
    }


def reference(x_prompt, x_sample, c_prompt, c_sample, cache_diff_k, cache_diff_v,
              cache_mla_ckv, cache_mla_krope, page_table, w_ada, b_ada, w_in, g_q,
              w_uq, w_uk, g_kv, lambda_q1, lambda_k1, lambda_q2, lambda_k2, g_sub,
              w_uv, w_out, w_ff1, w_ff2, g_final):
    pos_p = jnp.arange(x_prompt.shape[1])
    pos_s = PAST_LEN + jnp.arange(x_sample.shape[1])
    xp, xs = x_prompt, x_sample
    new_p = ([], [], [], [])
    new_s = ([], [], [], [])
    for l in range(DEPTH):
        lam_init = 0.8 - 0.6 * math.exp(-0.3 * l)
        lam = diff_lambda(lambda_q1[l], lambda_k1[l], lambda_q2[l], lambda_k2[l], lam_init)
        layer_w = (w_ada[l], b_ada[l], w_in[l], g_q[l], w_uq[l], w_uk[l], g_kv[l],
                   g_sub[l], w_uv[l], w_out[l], w_ff1[l], w_ff2[l], lam, lam_init)
        xp, rows_p = decoder_layer(xp, c_prompt, pos_p, prompt_attention, *layer_w)
        attend_s = lambda qa, ka, va, ql, qr, cn, kr, lm, _l=l: sample_attention(
            qa, ka, va, ql, qr, cn, kr, lm, page_table, cache_diff_k, cache_diff_v,
            cache_mla_ckv, cache_mla_krope, _l)
        xs, rows_s = decoder_layer(xs, c_sample, pos_s, attend_s, *layer_w)
        for acc, r in zip(new_p, rows_p):
            acc.append(r)
        for acc, r in zip(new_s, rows_s):
            acc.append(r)
    y_prompt = rms_norm(xp) * g_final
    y_sample = rms_norm(xs) * g_final
    k_diff_prompt = jnp.stack(new_p[0], axis=0)
    v_diff_prompt = jnp.stack(new_p[1], axis=0)
    ckv_prompt = jnp.stack(new_p[2], axis=0)
    krope_prompt = jnp.stack(new_p[3], axis=0)
    k_diff_sample = jnp.stack(new_s[0], axis=0)
    v_diff_sample = jnp.stack(new_s[1], axis=0)
    ckv_sample = jnp.stack(new_s[2], axis=0)
    krope_sample = jnp.stack(new_s[3], axis=0)
    return (y_prompt, y_sample, k_diff_prompt, v_diff_prompt, ckv_prompt, krope_prompt,
            k_diff_sample, v_diff_sample, ckv_sample, krope_sample)
```

```python
import functools
import math

import jax
import jax.numpy as jnp
from jax import lax
from jax.experimental import pallas as pl
from jax.experimental.pallas import tpu as pltpu

F32 = jnp.float32
BF16 = jnp.bfloat16

ROPE_THETA = 10000.0
EPS = 1e-6
NEG_INF = -1e30
LANES = 128
VMEM_LIMIT = 56 * 1024 * 1024


def _cparams(sem):
    return pltpu.CompilerParams(dimension_semantics=sem, vmem_limit_bytes=VMEM_LIMIT)


def _rms(x):
    return x * lax.rsqrt(jnp.mean(x * x, axis=-1, keepdims=True) + EPS)


def _dot(a, b):
    return jnp.dot(a, b, preferred_element_type=F32)


def _dot_nt(a, b):
    return lax.dot_general(a, b, (((1,), (1,)), ((), ())), preferred_element_type=F32)


def _slab(i, w=LANES):
    return slice(i * w, (i + 1) * w)


def _log2(n):
    assert n & (n - 1) == 0, "power of two expected"
    return n.bit_length() - 1


def _rope(x, cos, sin_signed, half):
    lane = lax.broadcasted_iota(jnp.int32, x.shape, 1)
    first = (lane & (2 * half - 1)) < half
    swapped = jnp.where(first, pltpu.roll(x, LANES - half, 1), pltpu.roll(x, half, 1))
    return x * cos + swapped * sin_signed


def _adaln_kernel(c_ref, w_ref, b_ref, o_ref):
    c = c_ref[...]
    s = c / (1.0 + jnp.exp(-c))
    o_ref[...] = _dot(s.astype(BF16), w_ref[...].astype(BF16)) + b_ref[...]


def _adaln(c_all, w_ada, b_ada):
    n, d = c_all.shape
    d_out = w_ada.shape[1]
    tn = 1024
    return pl.pallas_call(
        _adaln_kernel,
        grid=(d_out // tn,),
        in_specs=[pl.BlockSpec((n, d), lambda j: (0, 0)),
                  pl.BlockSpec((d, tn), lambda j: (0, j)),
                  pl.BlockSpec((1, tn), lambda j: (0, j))],
        out_specs=pl.BlockSpec((n, tn), lambda j: (0, j)),
        out_shape=jax.ShapeDtypeStruct((n, d_out), F32),
        compiler_params=_cparams(("arbitrary",)),
        name="adaln",
    )(c_all, w_ada, b_ada.reshape(1, d_out))


def _proj_kernel(cfg, prompt, x_ref, mod_ref, cosa_ref, sina_ref, cosb_ref, sinb_ref, win_ref,
                 gq_ref, wuq_ref, gkv_ref, wa_ref, wb_ref, *outs):
    h_a, dk_a, h_b, q_lora, kv_lora, scale_a, scale_b = cfg
    n_a = h_a * 2 * dk_a
    off_cq = 3 * n_a
    off_ckv = off_cq + q_lora
    off_kr = off_ckv + kv_lora
    x = x_ref[...]
    g, r, d = x.shape
    mod = mod_ref[...]
    hmod = _rms(x) * (1.0 + mod[:, 1:2, :]) + mod[:, 0:1, :]
    hmod = hmod.reshape(g * r, d).astype(BF16)
    p = _dot(hmod, win_ref[...])
    cosa, sina, cosb, sinb = cosa_ref[...], sina_ref[...], cosb_ref[...], sinb_ref[...]
    half_a = dk_a // 2
    half_b = 16

    cq = _rms(p[:, off_cq:off_ckv]) * gq_ref[...]
    qb = _dot(cq.astype(BF16), wuq_ref[...])
    ckv = _rms(p[:, off_ckv:off_kr]) * gkv_ref[...]
    kr = _rope(p[:, off_kr:off_kr + LANES], cosb, sinb, half_b)
    ckv_bf = ckv.astype(BF16)

    if prompt:
        qa_ref, ka_ref, kabf_ref, va_ref, vabf_ref, ckv_ref, kr_ref, qb_ref, kb_ref, vb_ref = outs
        for s in range(n_a // LANES):
            qa_ref[:, _slab(s)] = (_rope(p[:, _slab(s)], cosa, sina, half_a) * scale_a).astype(BF16)
            k = _rope(p[:, n_a + s * LANES:n_a + (s + 1) * LANES], cosa, sina, half_a)
            ka_ref[:, _slab(s)] = k
            kabf_ref[:, _slab(s)] = k.astype(BF16)
        v = p[:, 2 * n_a:3 * n_a]
        va_ref[...] = v
        vabf_ref[...] = v.astype(BF16)
        ckv_ref[...] = ckv
        kr_ref[...] = kr[:, 64:96]
        knope = _dot(ckv_bf, wa_ref[...])
        for h in range(h_b):
            qb_ref[:, _slab(h)] = (_rope(qb[:, _slab(h)], cosb, sinb, half_b) * scale_b).astype(BF16)
            kb_ref[:, _slab(h)] = (knope[:, _slab(h)] + kr).astype(BF16)
        vb_ref[...] = _dot(ckv_bf, wb_ref[...]).astype(BF16)
    else:
        qa_ref, ka_ref, va_ref, ckv_ref, kr_ref, qlat_ref, qrope_ref = outs
        for s in range(n_a // LANES):
            qa_ref[:, _slab(s)] = _rope(p[:, _slab(s)], cosa, sina, half_a) * scale_a
            ka_ref[:, _slab(s)] = _rope(p[:, n_a + s * LANES:n_a + (s + 1) * LANES], cosa, sina, half_a)
        va_ref[...] = p[:, 2 * n_a:3 * n_a]
        ckv_ref[...] = ckv
        kr_ref[...] = kr[:, 64:96]
        qbr = [(_rope(qb[:, _slab(h)], cosb, sinb, half_b) * scale_b).astype(BF16) for h in range(h_b)]
        for h in range(h_b):
            qlat_ref[:, h * kv_lora:(h + 1) * kv_lora] = _dot(qbr[h], wa_ref[h])
        qrope_ref[...] = _dot(jnp.concatenate(qbr, axis=1), wb_ref[...])


def _proj(cfg, prompt, x3, mod3, tables, weights, g_blk, tiles_per_mod, tiles_per_table):
    ng, r, d = x3.shape
    tm = g_blk * r
    n = ng * r
    h_a, dk_a, h_b, q_lora, kv_lora, _, _ = cfg
    n_a = h_a * 2 * dk_a
    cosa, sina, cosb, sinb = tables
    win, gq, wuq, gkv, wa, wb = weights
    full = lambda a: pl.BlockSpec(a.shape, lambda i, nd=a.ndim: (0,) * nd)
    tab = pl.BlockSpec((tm, LANES), lambda i: (i % tiles_per_table, 0))
    in_specs = [pl.BlockSpec((g_blk, r, d), lambda i: (i, 0, 0)),
                pl.BlockSpec((g_blk, 6, d), lambda i: (i // tiles_per_mod, 0, 0)),
                tab, tab, tab, tab,
                full(win), full(gq), full(wuq), full(gkv), full(wa), full(wb)]
    row = lambda w: pl.BlockSpec((tm, w), lambda i: (i, 0))
    sds = lambda w, dt: jax.ShapeDtypeStruct((n, w), dt)
    if prompt:
        widths = [(n_a, BF16), (n_a, F32), (n_a, BF16), (n_a, F32), (n_a, BF16), (kv_lora, F32), (32, F32),
                  (h_b * LANES, BF16), (h_b * LANES, BF16), (h_b * 64, BF16)]
    else:
        widths = [(n_a, F32), (n_a, F32), (n_a, F32), (kv_lora, F32), (32, F32),
                  (h_b * kv_lora, F32), (h_b * 32, F32)]
    return pl.pallas_call(
        functools.partial(_proj_kernel, cfg, prompt),
        grid=(ng // g_blk,),
        in_specs=in_specs,
        out_specs=[row(w) for w, _ in widths],
        out_shape=[sds(w, dt) for w, dt in widths],
        compiler_params=_cparams(("parallel",)),
        name="proj_prompt" if prompt else "proj_sample",
    )(x3, mod3, cosa, sina, cosb, sinb, win, gq, wuq, gkv, wa, wb)


def _lambda(lamv, lam_init):
    e1 = jnp.exp(jnp.sum(lamv[0:1] * lamv[1:2], axis=-1, keepdims=True))
    e2 = jnp.exp(jnp.sum(lamv[2:3] * lamv[3:4], axis=-1, keepdims=True))
    return e1 - e2 + lam_init


def _prompt_attn_kernel(h_a, h_b, lam_init, tq, qa_ref, qb_ref, ka_ref, va_ref, kb_ref, vb_ref, lamv_ref,
                        oa_ref, ob_ref):
    qi = pl.program_id(1)
    lam = _lambda(lamv_ref[...], lam_init)
    row = lax.broadcasted_iota(jnp.int32, (tq, tq), 0)
    col = lax.broadcasted_iota(jnp.int32, (tq, tq), 1)
    causal = col <= row
    lane = lax.broadcasted_iota(jnp.int32, (tq, LANES), 1)
    low = lane < 64

    def flash(q, k_ref, ks, v_ref, vs):
        def step(j, carry, masked):
            m, l, acc = carry
            k0 = pl.multiple_of(j * tq, tq)
            k = k_ref[pl.ds(k0, tq), _slab(ks)]
            v = v_ref[pl.ds(k0, tq), _slab(vs)]
            sc = _dot_nt(q, k)
            if masked:
                sc = jnp.where(causal, sc, NEG_INF)
            m_new = jnp.maximum(m, jnp.max(sc, axis=-1, keepdims=True))
            alpha = jnp.exp(m - m_new)
            p = jnp.exp(sc - m_new)
            l = alpha * l + jnp.sum(p, axis=-1, keepdims=True)
            acc = alpha * acc + _dot(p.astype(BF16), v)
            return m_new, l, acc
        init = (jnp.full((tq, 1), NEG_INF, F32), jnp.zeros((tq, 1), F32), jnp.zeros((tq, LANES), F32))
        carry = lax.fori_loop(0, qi, lambda j, c: step(j, c, False), init)
        _, l, acc = step(qi, carry, True)
        return acc / l

    zero = jnp.zeros((tq, LANES), BF16)
    for h in range(h_a):
        q = qa_ref[:, _slab(h)]
        o1 = flash(jnp.where(low, q, zero), ka_ref, h, va_ref, h)
        o2 = flash(jnp.where(low, zero, q), ka_ref, h, va_ref, h)
        oa_ref[:, _slab(h)] = o1 - lam * o2
    for hp in range(h_b // 2):
        oe = flash(qb_ref[:, _slab(2 * hp)], kb_ref, 2 * hp, vb_ref, hp)
        oo = flash(qb_ref[:, _slab(2 * hp + 1)], kb_ref, 2 * hp + 1, vb_ref, hp)
        ob_ref[:, _slab(hp)] = jnp.where(low, oe, oo).astype(BF16)


def _prompt_attn(h_a, h_b, lam_init, batch, seq, tq, qa, qb, ka, va, kb, vb, lamv):
    nq = seq // tq
    n = batch * seq
    qspec = lambda w: pl.BlockSpec((tq, w), lambda b, i: (b * nq + i, 0))
    kspec = lambda w: pl.BlockSpec((seq, w), lambda b, i: (b, 0))
    return pl.pallas_call(
        functools.partial(_prompt_attn_kernel, h_a, h_b, lam_init, tq),
        grid=(batch, nq),
        in_specs=[qspec(qa.shape[1]), qspec(qb.shape[1]), kspec(ka.shape[1]), kspec(va.shape[1]),
                  kspec(kb.shape[1]), kspec(vb.shape[1]),
                  pl.BlockSpec(lamv.shape, lambda b, i: (0, 0))],
        out_specs=[qspec(h_a * LANES), qspec(h_b * 64)],
        out_shape=[jax.ShapeDtypeStruct((n, h_a * LANES), F32), jax.ShapeDtypeStruct((n, h_b * 64), BF16)],
        compiler_params=_cparams(("parallel", "arbitrary")),
        name="prompt_attn",
    )(qa, qb, ka, va, kb, vb, lamv)


def _sample_attn_kernel(h_a, h_b, kv_lora, lam_init, npg, pt_ref, qa_ref, qlat_ref, qrope_ref, kn_ref, vn_ref,
                        cn_ref, rn_ref, lamv_ref, wuv_ref, *rest):
    kpg, vpg, cpg, rpg = (rest[i * npg:(i + 1) * npg] for i in range(4))
    oa_ref, ob_ref = rest[4 * npg:4 * npg + 2]
    qbd, ql, qr, m_a, l_a, acc_a, m_b, l_b, acc_b = rest[4 * npg + 2:]
    j = pl.program_id(1)
    t = qa_ref.shape[0]
    page = kpg[0].shape[0]
    rows_a = 2 * h_a * t
    rows_b = h_b * t

    def update(m_ref, l_ref, acc_ref, s, vals):
        m_old = m_ref[...]
        m_new = jnp.maximum(m_old, jnp.max(s, axis=-1, keepdims=True))
        alpha = jnp.exp(m_old - m_new)
        p = jnp.exp(s - m_new)
        l_ref[...] = alpha * l_ref[...] + jnp.sum(p, axis=-1, keepdims=True)
        pv = _dot(p[:, 0:page].astype(BF16), vals[0])
        for i in range(1, len(vals)):
            pv = pv + _dot(p[:, i * page:(i + 1) * page].astype(BF16), vals[i])
        acc_ref[...] = alpha * acc_ref[...] + pv
        m_ref[...] = m_new

    def attend(ks, vs, cs, rs, mask_a, mask_b):
        sa = jnp.concatenate([_dot_nt(qbd[...], k) for k in ks], axis=1)
        sb = jnp.concatenate([_dot_nt(ql[...], c) + _dot_nt(qr[...], r) for c, r in zip(cs, rs)], axis=1)
        if mask_a is not None:
            sa = jnp.where(mask_a, sa, NEG_INF)
            sb = jnp.where(mask_b, sb, NEG_INF)
        update(m_a, l_a, acc_a, sa, vs)
        update(m_b, l_b, acc_b, sb, cs)

    @pl.when(j == 0)
    def _():
        qa = qa_ref[...]
        r = lax.broadcasted_iota(jnp.int32, (rows_a, qa.shape[1]), 0)
        c = lax.broadcasted_iota(jnp.int32, (rows_a, qa.shape[1]), 1)
        dk = qa.shape[1] // (2 * h_a)
        qbd[...] = jnp.where((r >> _log2(t)) == (c >> _log2(dk)),
                             jnp.concatenate([qa] * (2 * h_a), axis=0), 0.0).astype(BF16)
        ql[...] = jnp.concatenate([qlat_ref[:, h * kv_lora:(h + 1) * kv_lora] for h in range(h_b)],
                                  axis=0).astype(BF16)
        qr[...] = jnp.concatenate([qrope_ref[:, h * 32:(h + 1) * 32] for h in range(h_b)], axis=0).astype(BF16)
        for m_ref, l_ref, acc_ref in ((m_a, l_a, acc_a), (m_b, l_b, acc_b)):
            m_ref[...] = jnp.full(m_ref.shape, NEG_INF, F32)
            l_ref[...] = jnp.zeros(l_ref.shape, F32)
            acc_ref[...] = jnp.zeros(acc_ref.shape, F32)
        pad = lambda ref: jnp.concatenate(
            [ref[...], jnp.zeros((page - t, ref.shape[1]), F32)], axis=0).astype(BF16)
        ca = lax.broadcasted_iota(jnp.int32, (rows_a, page), 1)
        ra = lax.broadcasted_iota(jnp.int32, (rows_a, page), 0)
        cb = lax.broadcasted_iota(jnp.int32, (rows_b, page), 1)
        rb = lax.broadcasted_iota(jnp.int32, (rows_b, page), 0)
        attend([pad(kn_ref)], [pad(vn_ref)], [pad(cn_ref)], [pad(rn_ref)],
               ca <= (ra & (t - 1)), cb <= (rb & (t - 1)))

    attend([k[...].astype(BF16) for k in kpg], [v[...].astype(BF16) for v in vpg],
           [c[...].astype(BF16) for c in cpg], [r[...].astype(BF16) for r in rpg], None, None)

    @pl.when(j == pl.num_programs(1) - 1)
    def _():
        lam = _lambda(lamv_ref[...], lam_init)
        oa = acc_a[...] / l_a[...]
        for h in range(h_a):
            o1 = oa[(2 * h) * t:(2 * h + 1) * t, _slab(h)]
            o2 = oa[(2 * h + 1) * t:(2 * h + 2) * t, _slab(h)]
            oa_ref[:, _slab(h)] = o1 - lam * o2
        olat = (acc_b[...] / l_b[...]).astype(BF16)
        ofull = _dot(olat, wuv_ref[...])
        cc = lax.broadcasted_iota(jnp.int32, (t, ofull.shape[1]), 1) >> 6
        ob = jnp.zeros((t, ofull.shape[1]), F32)
        for h in range(h_b):
            ob = jnp.where(cc == h, ofull[h * t:(h + 1) * t, :], ob)
        ob_ref[...] = ob


def _sample_attn(h_a, h_b, kv_lora, lam_init, npg, layer, page_table, qa3, qlat3, qrope3, kn3, vn3, cn3, rn3,
                 lamv, wuv, cache_k, cache_v, cache_c, cache_r):
    s, t, n_a = qa3.shape
    n_pages = page_table.shape[1]
    n_phys, page = cache_k.shape[1], cache_k.shape[2]
    steps = n_pages // npg
    base = layer * n_phys
    seq = lambda a: pl.BlockSpec((None,) + a.shape[1:], lambda i, j, pt: (i, 0, 0))
    full = lambda a: pl.BlockSpec(a.shape, lambda i, j, pt, nd=a.ndim: (0,) * nd)
    paged = lambda a, k: pl.BlockSpec((None,) + a.shape[1:],
                                      lambda i, j, pt, k=k: (base + pt[i, j * npg + k], 0, 0))
    caches = (cache_k, cache_v, cache_c, cache_r)
    in_specs = [seq(qa3), seq(qlat3), seq(qrope3), seq(kn3), seq(vn3), seq(cn3), seq(rn3), full(lamv), full(wuv)]
    in_specs += [paged(a, k) for a in caches for k in range(npg)]
    rows_a, rows_b = 2 * h_a * t, h_b * t
    dv_all = cache_v.shape[2]
    scratch = [pltpu.VMEM((rows_a, n_a), BF16), pltpu.VMEM((rows_b, kv_lora), BF16), pltpu.VMEM((rows_b, 32), BF16),
               pltpu.VMEM((rows_a, 1), F32), pltpu.VMEM((rows_a, 1), F32), pltpu.VMEM((rows_a, dv_all), F32),
               pltpu.VMEM((rows_b, 1), F32), pltpu.VMEM((rows_b, 1), F32), pltpu.VMEM((rows_b, kv_lora), F32)]
    out_w = h_b * 64
    grid_spec = pltpu.PrefetchScalarGridSpec(
        num_scalar_prefetch=1,
        grid=(s, steps),
        in_specs=in_specs,
        out_specs=[pl.BlockSpec((None, t, dv_all), lambda i, j, pt: (i, 0, 0)),
                   pl.BlockSpec((None, t, out_w), lambda i, j, pt: (i, 0, 0))],
        scratch_shapes=scratch)
    args = [qa3, qlat3, qrope3, kn3, vn3, cn3, rn3, lamv, wuv]
    args += [a for a in caches for _ in range(npg)]
    return pl.pallas_call(
        functools.partial(_sample_attn_kernel, h_a, h_b, kv_lora, lam_init, npg),
        grid_spec=grid_spec,
        out_shape=[jax.ShapeDtypeStruct((s, t, dv_all), F32), jax.ShapeDtypeStruct((s, t, out_w), F32)],
        compiler_params=_cparams(("parallel", "arbitrary")),
        name="sample_attn",
    )(page_table, *args)


def _mix_ffn_kernel(h_a, out_scale, ff_chunk, final, x_ref, mod_ref, oa_ref, ob_ref, gsub_ref, wout_ref,
                    w1_ref, w2_ref, gfin_ref, y_ref):
    x = x_ref[...]
    g, r, d = x.shape
    mod = mod_ref[...]
    gsub = gsub_ref[...] * out_scale
    parts = [(_rms(oa_ref[:, _slab(h)]) * gsub).astype(BF16) for h in range(h_a)]
    parts.append(ob_ref[...].astype(BF16))
    att = _dot(jnp.concatenate(parts, axis=1), wout_ref[...])
    x1 = x + mod[:, 2:3, :] * att.reshape(g, r, d)
    hm = (_rms(x1) * (1.0 + mod[:, 4:5, :]) + mod[:, 3:4, :]).reshape(g * r, d).astype(BF16)
    d_ff = w1_ref.shape[1]
    ff = jnp.zeros((g * r, d), F32)
    for c in range(d_ff // ff_chunk):
        a = _dot(hm, w1_ref[:, c * ff_chunk:(c + 1) * ff_chunk])
        a = jnp.square(jnp.maximum(a, 0.0)).astype(BF16)
        ff = ff + _dot(a, w2_ref[c * ff_chunk:(c + 1) * ff_chunk, :])
    x2 = x1 + mod[:, 5:6, :] * ff.reshape(g, r, d)
    if final:
        x2 = _rms(x2) * gfin_ref[...]
    y_ref[...] = x2


def _mix_ffn(h_a, out_scale, final, x3, mod3, oa, ob, gsub, wout, w1, w2, gfin, g_blk, tiles_per_mod):
    ng, r, d = x3.shape
    tm = g_blk * r
    full = lambda a: pl.BlockSpec(a.shape, lambda i, nd=a.ndim: (0,) * nd)
    row = lambda a: pl.BlockSpec((tm, a.shape[1]), lambda i: (i, 0))
    return pl.pallas_call(
        functools.partial(_mix_ffn_kernel, h_a, out_scale, 1024, final),
        grid=(ng // g_blk,),
        in_specs=[pl.BlockSpec((g_blk, r, d), lambda i: (i, 0, 0)),
                  pl.BlockSpec((g_blk, 6, d), lambda i: (i // tiles_per_mod, 0, 0)),
                  row(oa), row(ob), full(gsub), full(wout), full(w1), full(w2), full(gfin)],
        out_specs=pl.BlockSpec((g_blk, r, d), lambda i: (i, 0, 0)),
        out_shape=jax.ShapeDtypeStruct((ng, r, d), F32),
        compiler_params=_cparams(("parallel",)),
        name="mix_ffn",
    )(x3, mod3, oa, ob, gsub, wout, w1, w2, gfin)


def _rope_tables(pos, d_a, d_b):
    def cs(d):
        inv = 1.0 / (ROPE_THETA ** (jnp.arange(0, d, 2, dtype=F32) / d))
        ang = pos.astype(F32)[:, None] * inv[None, :]
        c, s = jnp.cos(ang), jnp.sin(ang)
        return jnp.concatenate([c, c], axis=1), jnp.concatenate([-s, s], axis=1)
    ca, sa = cs(d_a)
    cb, sb = cs(d_b)
    n = pos.shape[0]
    reps = LANES // d_a
    cosa, sina = jnp.tile(ca, (1, reps)), jnp.tile(sa, (1, reps))
    one, zero = jnp.ones((n, 64), F32), jnp.zeros((n, 64), F32)
    cosb = jnp.concatenate([one, cb, one[:, :LANES - 64 - d_b]], axis=1)
    sinb = jnp.concatenate([zero, sb, zero[:, :LANES - 64 - d_b]], axis=1)
    return cosa, sina, cosb, sinb


def kernel(x_prompt, x_sample, c_prompt, c_sample, cache_diff_k, cache_diff_v, cache_mla_ckv, cache_mla_krope,
           page_table, w_ada, b_ada, w_in, g_q, w_uq, w_uk, g_kv, lambda_q1, lambda_k1, lambda_q2, lambda_k2,
           g_sub, w_uv, w_out, w_ff1, w_ff2, g_final):
    batch, seq, d = x_prompt.shape
    s_batch, s_seq, _ = x_sample.shape
    depth, n_phys, page, h_a, _, dk_a = cache_diff_k.shape
    dv_a = cache_diff_v.shape[-1]
    kv_lora = cache_mla_ckv.shape[-1]
    qk_rope = cache_mla_krope.shape[-1]
    h_b, _, qk_nope = w_uk.shape[1:]
    v_head = w_uv.shape[-1]
    q_lora = g_q.shape[-1]
    past_len = page_table.shape[1] * page
    n_a = h_a * 2 * dk_a
    assert (dk_a, dv_a, qk_nope, qk_rope, v_head) == (64, 128, 64, 32, 64), "head geometry the lane layouts assume"
    assert w_in.shape[-1] == 3 * n_a + q_lora + kv_lora + qk_rope
    cfg = (h_a, dk_a, h_b, q_lora, kv_lora, dk_a ** -0.5, (qk_nope + qk_rope) ** -0.5)

    tm_p, tq, ts, npg = 256, 256, 32, 8
    tab_p = _rope_tables(jnp.arange(seq), dk_a, qk_rope)
    tab_s = tuple(jnp.tile(a, (ts, 1)) for a in _rope_tables(past_len + jnp.arange(s_seq), dk_a, qk_rope))

    xp3 = x_prompt.reshape(batch * seq // tm_p, tm_p, d)
    xs3 = x_sample
    c_all = jnp.concatenate([c_prompt, c_sample], axis=0)
    ck = cache_diff_k.reshape(depth * n_phys, page, n_a)
    cv = cache_diff_v.reshape(depth * n_phys, page, h_a * dv_a)
    cc = cache_mla_ckv.reshape(depth * n_phys, page, kv_lora)
    cr = cache_mla_krope.reshape(depth * n_phys, page, qk_rope)

    sel_rows = (jnp.arange(h_b)[:, None] * LANES + 64 + jnp.arange(qk_rope)[None, :]).reshape(-1)
    sel = jnp.zeros((h_b * LANES, h_b * qk_rope), BF16).at[sel_rows, jnp.arange(h_b * qk_rope)].set(1.0)

    new_p, new_s = [], []
    for l in range(depth):
        lam_init = 0.8 - 0.6 * math.exp(-0.3 * l)
        mod = _adaln(c_all, w_ada[l], b_ada[l]).reshape(batch + s_batch, 6, d)
        mod_p, mod_s = mod[:batch], mod[batch:]
        win = jnp.concatenate([w_in[l][:, :3 * n_a + q_lora + kv_lora], jnp.zeros((d, 64), F32),
                               w_in[l][:, 3 * n_a + q_lora + kv_lora:], jnp.zeros((d, LANES - 64 - qk_rope), F32)],
                              axis=1).astype(BF16)
        wuq = jnp.pad(w_uq[l].reshape(q_lora, h_b, qk_nope + qk_rope),
                      ((0, 0), (0, 0), (0, LANES - qk_nope - qk_rope))).reshape(q_lora, h_b * LANES).astype(BF16)
        wuk_pad = jnp.pad(jnp.transpose(w_uk[l], (1, 0, 2)),
                          ((0, 0), (0, 0), (0, LANES - qk_nope))).reshape(kv_lora, h_b * LANES).astype(BF16)
        wuv_all = jnp.transpose(w_uv[l], (1, 0, 2)).reshape(kv_lora, h_b * v_head).astype(BF16)
        wabs = jnp.pad(jnp.transpose(w_uk[l], (0, 2, 1)), ((0, 0), (0, LANES - qk_nope), (0, 0))).astype(BF16)
        gq, gkv, gsub = g_q[l].reshape(1, -1), g_kv[l].reshape(1, -1), g_sub[l].reshape(1, -1)
        lamv = jnp.stack([lambda_q1[l], lambda_k1[l], lambda_q2[l], lambda_k2[l]], axis=0)
        wout, w1, w2 = w_out[l].astype(BF16), w_ff1[l].astype(BF16), w_ff2[l].astype(BF16)
        gfin = g_final.reshape(1, d)
        final = l == depth - 1

        qa, ka, ka_bf, va, va_bf, ckv, kr, qb, kb, vb = _proj(
            cfg, True, xp3, mod_p, tab_p, (win, gq, wuq, gkv, wuk_pad, wuv_all), 1, seq // tm_p, seq // tm_p)
        oa, ob = _prompt_attn(h_a, h_b, lam_init, batch, seq, tq, qa, qb, ka_bf, va_bf, kb, vb, lamv)
        xp3 = _mix_ffn(h_a, 1.0 - lam_init, final, xp3, mod_p, oa, ob, gsub, wout, w1, w2, gfin, 1, seq // tm_p)
        new_p.append((ka.reshape(batch, seq, h_a, 2, dk_a), va.reshape(batch, seq, h_a, dv_a),
                      ckv.reshape(batch, seq, kv_lora), kr.reshape(batch, seq, qk_rope)))

        sqa, ska, sva, sckv, skr, sqlat, sqrope = _proj(
            cfg, False, xs3, mod_s, tab_s, (win, gq, wuq, gkv, wabs, sel), ts, 1, 1)
        r3 = lambda a: a.reshape(s_batch, s_seq, a.shape[-1])
        soa, sob = _sample_attn(h_a, h_b, kv_lora, lam_init, npg, l, page_table, r3(sqa), r3(sqlat), r3(sqrope),
                                r3(ska), r3(sva), r3(sckv), r3(skr), lamv, wuv_all, ck, cv, cc, cr)
        n_s = s_batch * s_seq
        xs3 = _mix_ffn(h_a, 1.0 - lam_init, final, xs3, mod_s, soa.reshape(n_s, -1), sob.reshape(n_s, -1),
                       gsub, wout, w1, w2, gfin, ts, 1)
        new_s.append((ska.reshape(s_batch, s_seq, h_a, 2, dk_a), sva.reshape(s_batch, s_seq, h_a, dv_a),
                      sckv.reshape(s_batch, s_seq, kv_lora), skr.reshape(s_batch, s_seq, qk_rope)))

    y_prompt = xp3.reshape(batch, seq, d)
    y_sample = xs3
    stack = lambda rows, i: jnp.stack([r[i] for r in rows], axis=0)
    return (y_prompt, y_sample, stack(new_p, 0), stack(new_p, 1), stack(new_p, 2), stack(new_p, 3),
            stack(new_s, 0), stack(new_s, 1), stack(new_s, 2), stack(new_s, 3))
```

```python
import functools
import math

import jax
import jax.numpy as jnp
from jax import lax
from jax.experimental import pallas as pl
from jax.experimental.pallas import tpu as pltpu

F32 = jnp.float32
BF16 = jnp.bfloat16

ROPE_THETA = 10000.0
EPS = 1e-6
NEG_INF = -1e30
LANES = 128
VMEM_LIMIT = 56 * 1024 * 1024


def _cparams(sem):
    return pltpu.CompilerParams(dimension_semantics=sem, vmem_limit_bytes=VMEM_LIMIT)


def _rms(x):
    return x * lax.rsqrt(jnp.mean(x * x, axis=-1, keepdims=True) + EPS)


def _dot(a, b):
    return jnp.dot(a, b, preferred_element_type=F32)


def _dot_nt(a, b):
    return lax.dot_general(a, b, (((1,), (1,)), ((), ())), preferred_element_type=F32)


def _slab(i, w=LANES):
    return slice(i * w, (i + 1) * w)


def _log2(n):
    assert n & (n - 1) == 0, "power of two expected"
    return n.bit_length() - 1


def _rope(x, cos, sin_signed, half):
    lane = lax.broadcasted_iota(jnp.int32, x.shape, 1)
    first = (lane & (2 * half - 1)) < half
    swapped = jnp.where(first, pltpu.roll(x, LANES - half, 1), pltpu.roll(x, half, 1))
    return x * cos + swapped * sin_signed


def _rope_rows(x1, x2, cos, sin):
    return x1 * cos - x2 * sin, x2 * cos + x1 * sin


def _lambda(lamv, lam_init):
    e1 = jnp.exp(jnp.sum(lamv[0:1] * lamv[1:2], axis=-1, keepdims=True))
    e2 = jnp.exp(jnp.sum(lamv[2:3] * lamv[3:4], axis=-1, keepdims=True))
    return e1 - e2 + lam_init


def _full(a):
    return pl.BlockSpec(a.shape, lambda *idx, nd=a.ndim: (0,) * nd)


def _adaln_kernel(c_ref, w_ref, b_ref, o_ref):
    c = c_ref[...]
    s = c / (1.0 + jnp.exp(-c))
    o_ref[...] = _dot(s.astype(BF16), w_ref[...].astype(BF16)) + b_ref[...]


def _adaln(c_all, w_ada, b_ada):
    n, d = c_all.shape
    d_out = w_ada.shape[1]
    tn = 1024
    return pl.pallas_call(
        _adaln_kernel,
        grid=(d_out // tn,),
        in_specs=[pl.BlockSpec((n, d), lambda j: (0, 0)),
                  pl.BlockSpec((d, tn), lambda j: (0, j)),
                  pl.BlockSpec((1, tn), lambda j: (0, j))],
        out_specs=pl.BlockSpec((n, tn), lambda j: (0, j)),
        out_shape=jax.ShapeDtypeStruct((n, d_out), F32),
        compiler_params=_cparams(("arbitrary",)),
        name="adaln",
    )(c_all, w_ada, b_ada.reshape(1, d_out))


def _modulated(x_ref, mod_ref):
    x = x_ref[...]
    g, r, d = x.shape
    mod = mod_ref[...]
    h = _rms(x) * (1.0 + mod[:, 1:2, :]) + mod[:, 0:1, :]
    return h.reshape(g * r, d).astype(BF16)


def _proj_prompt_kernel(cfg, x_ref, mod_ref, cosa_ref, sina_ref, cosb_ref, sinb_ref, cosat_ref, sinat_ref,
                        cosbt_ref, sinbt_ref, wmain_ref, wkt_ref, gq_ref, wuq_ref, gkv_ref, wukt_ref, wuv_ref,
                        q_ref, kat_ref, krt_ref, kt_ref, v_ref, va_ref, ckv_ref):
    h_a, dk_a, h_b, q_lora, kv_lora, scale_a, scale_b = cfg
    n_a = h_a * 2 * dk_a
    hmod = _modulated(x_ref, mod_ref)
    tm = hmod.shape[0]
    p = _dot(hmod, wmain_ref[...])
    kk = _dot_nt(wkt_ref[...], hmod)
    cosa, sina, cosb, sinb = cosa_ref[...], sina_ref[...], cosb_ref[...], sinb_ref[...]
    lane = lax.broadcasted_iota(jnp.int32, (tm, LANES), 1)
    low = lane < dk_a

    for h in range(h_a):
        q = _rope(p[:, _slab(h)], cosa, sina, dk_a // 2) * scale_a
        q_ref[:, _slab(2 * h)] = jnp.where(low, q, 0.0).astype(BF16)
        q_ref[:, _slab(2 * h + 1)] = jnp.where(low, 0.0, q).astype(BF16)
    cq = _rms(p[:, 2 * n_a:2 * n_a + q_lora]) * gq_ref[...]
    qb = _dot(cq.astype(BF16), wuq_ref[...])
    for h in range(h_b):
        q_ref[:, _slab(2 * h_a + h)] = (_rope(qb[:, _slab(h)], cosb, sinb, 16) * scale_b).astype(BF16)

    ca, sa = cosat_ref[...], sinat_ref[...]
    hd = dk_a // 2
    for g in range(n_a // dk_a):
        o1, o2 = _rope_rows(kk[g * dk_a:g * dk_a + hd], kk[g * dk_a + hd:(g + 1) * dk_a], ca, sa)
        kat_ref[g * dk_a:g * dk_a + hd, :] = o1
        kat_ref[g * dk_a + hd:(g + 1) * dk_a, :] = o2
        kt_ref[g * dk_a:g * dk_a + hd, :] = o1.astype(BF16)
        kt_ref[g * dk_a + hd:(g + 1) * dk_a, :] = o2.astype(BF16)
    r1, r2 = _rope_rows(kk[n_a:n_a + 16], kk[n_a + 16:n_a + 32], cosbt_ref[...], sinbt_ref[...])
    krt_ref[0:16, :] = r1
    krt_ref[16:32, :] = r2

    v = p[:, n_a:2 * n_a]
    for h in range(h_a):
        va_ref[pl.ds(h, tm, stride=h_a), :] = v[:, _slab(h)]
    v_ref[:, 0:n_a] = v.astype(BF16)
    ckv = _rms(p[:, 2 * n_a + q_lora:]) * gkv_ref[...]
    ckv_ref[...] = ckv
    ckv_bf = ckv.astype(BF16)
    v_ref[:, n_a:] = _dot(ckv_bf, wuv_ref[...]).astype(BF16)

    knt = _dot_nt(wukt_ref[...], ckv_bf)
    r1b, r2b = r1.astype(BF16), r2.astype(BF16)
    zero = jnp.zeros((32, tm), BF16)
    for h in range(h_b):
        base = n_a + h * LANES
        kt_ref[base:base + 64, :] = knt[h * 64:(h + 1) * 64].astype(BF16)
        kt_ref[base + 64:base + 80, :] = r1b
        kt_ref[base + 80:base + 96, :] = r2b
        kt_ref[base + 96:base + 128, :] = zero


def _proj_prompt(cfg, x3, mod3, tables, weights, batch, seq):
    ng, tm, d = x3.shape
    n = ng * tm
    tpb = seq // tm
    h_a, dk_a, h_b, q_lora, kv_lora, _, _ = cfg
    n_a = h_a * 2 * dk_a
    tab = pl.BlockSpec((tm, LANES), lambda i: (i % tpb, 0))
    tabt = lambda a: pl.BlockSpec((a.shape[0], tm), lambda i: (0, i % tpb))
    in_specs = [pl.BlockSpec((1, tm, d), lambda i: (i, 0, 0)),
                pl.BlockSpec((1, 6, d), lambda i: (i // tpb, 0, 0)),
                tab, tab, tab, tab] + [tabt(a) for a in tables[4:]] + [_full(w) for w in weights]
    row = lambda w: pl.BlockSpec((tm, w), lambda i: (i, 0))
    nq = (2 * h_a + h_b) * LANES
    nk = n_a + h_b * LANES
    nv = n_a + h_b * 64
    out_specs = [row(nq),
                 pl.BlockSpec((None, n_a, tm), lambda i: (i // tpb, 0, i % tpb)),
                 pl.BlockSpec((None, 32, tm), lambda i: (i // tpb, 0, i % tpb)),
                 pl.BlockSpec((None, None, nk, tm), lambda i: (i // tpb, i % tpb, 0, 0)),
                 row(nv),
                 pl.BlockSpec((tm * h_a, LANES), lambda i: (i, 0)),
                 row(kv_lora)]
    out_shape = [jax.ShapeDtypeStruct((n, nq), BF16),
                 jax.ShapeDtypeStruct((batch, n_a, seq), F32),
                 jax.ShapeDtypeStruct((batch, 32, seq), F32),
                 jax.ShapeDtypeStruct((batch, tpb, nk, tm), BF16),
                 jax.ShapeDtypeStruct((n, nv), BF16),
                 jax.ShapeDtypeStruct((n * h_a, LANES), F32),
                 jax.ShapeDtypeStruct((n, kv_lora), F32)]
    return pl.pallas_call(
        functools.partial(_proj_prompt_kernel, cfg),
        grid=(ng,),
        in_specs=in_specs,
        out_specs=out_specs,
        out_shape=out_shape,
        compiler_params=_cparams(("parallel",)),
        name="proj_prompt",
    )(x3, mod3, *tables, *weights)


def _proj_sample_kernel(cfg, x_ref, mod_ref, cosa_ref, sina_ref, cosb_ref, sinb_ref, win_ref, gq_ref, wuq_ref,
                        gkv_ref, wabs_ref, sel_ref, qa_ref, ka_ref, va_ref, ckv_ref, kr_ref, qlat_ref, qrope_ref):
    h_a, dk_a, h_b, q_lora, kv_lora, scale_a, scale_b = cfg
    n_a = h_a * 2 * dk_a
    off_cq = 3 * n_a
    off_ckv = off_cq + q_lora
    off_kr = off_ckv + kv_lora
    hmod = _modulated(x_ref, mod_ref)
    p = _dot(hmod, win_ref[...])
    cosa, sina, cosb, sinb = cosa_ref[...], sina_ref[...], cosb_ref[...], sinb_ref[...]
    for s in range(n_a // LANES):
        qa_ref[:, _slab(s)] = _rope(p[:, _slab(s)], cosa, sina, dk_a // 2) * scale_a
        ka_ref[:, _slab(s)] = _rope(p[:, n_a + s * LANES:n_a + (s + 1) * LANES], cosa, sina, dk_a // 2)
    va_ref[...] = p[:, 2 * n_a:3 * n_a]
    cq = _rms(p[:, off_cq:off_ckv]) * gq_ref[...]
    qb = _dot(cq.astype(BF16), wuq_ref[...])
    ckv_ref[...] = _rms(p[:, off_ckv:off_kr]) * gkv_ref[...]
    kr_ref[...] = _rope(p[:, off_kr:off_kr + LANES], cosb, sinb, 16)[:, 64:96]
    qbr = [(_rope(qb[:, _slab(h)], cosb, sinb, 16) * scale_b).astype(BF16) for h in range(h_b)]
    for h in range(h_b):
        qlat_ref[:, h * kv_lora:(h + 1) * kv_lora] = _dot(qbr[h], wabs_ref[h])
    qrope_ref[...] = _dot(jnp.concatenate(qbr, axis=1), sel_ref[...])


def _proj_sample(cfg, x3, mod3, tables, weights, g_blk):
    ng, r, d = x3.shape
    tm = g_blk * r
    n = ng * r
    h_a, dk_a, h_b, q_lora, kv_lora, _, _ = cfg
    n_a = h_a * 2 * dk_a
    tab = pl.BlockSpec((tm, LANES), lambda i: (0, 0))
    in_specs = [pl.BlockSpec((g_blk, r, d), lambda i: (i, 0, 0)),
                pl.BlockSpec((g_blk, 6, d), lambda i: (i, 0, 0)),
                tab, tab, tab, tab] + [_full(w) for w in weights]
    widths = [n_a, n_a, n_a, kv_lora, 32, h_b * kv_lora, h_b * 32]
    return pl.pallas_call(
        functools.partial(_proj_sample_kernel, cfg),
        grid=(ng // g_blk,),
        in_specs=in_specs,
        out_specs=[pl.BlockSpec((tm, w), lambda i: (i, 0)) for w in widths],
        out_shape=[jax.ShapeDtypeStruct((n, w), F32) for w in widths],
        compiler_params=_cparams(("parallel",)),
        name="proj_sample",
    )(x3, mod3, *tables, *weights)


def _prompt_attn_kernel(h_a, h_b, lam_init, tq, q_ref, kt_ref, v_ref, lamv_ref, oa_ref, ob_ref,
                        m_ref, l_ref, acc_ref):
    qi = pl.program_id(1)
    n_streams = 2 * h_a + h_b
    row = lax.broadcasted_iota(jnp.int32, (tq, tq), 0)
    col = lax.broadcasted_iota(jnp.int32, (tq, tq), 1)
    causal = col <= row

    def slabs(st):
        if st < 2 * h_a:
            return st // 2, st // 2
        h = st - 2 * h_a
        return h_a + h, h_a + h // 2

    m_ref[...] = jnp.full(m_ref.shape, NEG_INF, F32)
    l_ref[...] = jnp.zeros(l_ref.shape, F32)
    acc_ref[...] = jnp.zeros(acc_ref.shape, F32)

    def block(j, masked):
        k0 = pl.multiple_of(j * tq, tq)
        for st in range(n_streams):
            ks, vs = slabs(st)
            s = _dot(q_ref[:, _slab(st)], kt_ref[j, _slab(ks), :])
            if masked:
                s = jnp.where(causal, s, NEG_INF)
            m_old = m_ref[st]
            m_new = jnp.maximum(m_old, jnp.max(s, axis=-1, keepdims=True))
            alpha = jnp.exp(m_old - m_new)
            p = jnp.exp(s - jnp.concatenate([m_new] * (tq // LANES), axis=1))
            psum = p[:, 0:LANES]
            for c in range(1, tq // LANES):
                psum = psum + p[:, _slab(c)]
            l_ref[st] = alpha * l_ref[st] + psum
            m_ref[st] = m_new
            acc_ref[st] = alpha * acc_ref[st] + _dot(p.astype(BF16), v_ref[pl.ds(k0, tq), _slab(vs)])

    def body(j, carry):
        block(j, False)
        return carry
    lax.fori_loop(0, qi, body, 0)
    block(qi, True)

    lam = _lambda(lamv_ref[...], lam_init)
    out = lambda st: acc_ref[st] / jnp.sum(l_ref[st], axis=-1, keepdims=True)
    for h in range(h_a):
        oa_ref[:, _slab(h)] = out(2 * h) - lam * out(2 * h + 1)
    low = lax.broadcasted_iota(jnp.int32, (tq, LANES), 1) < 64
    for hp in range(h_b // 2):
        ob_ref[:, _slab(hp)] = jnp.where(low, out(2 * h_a + 2 * hp), out(2 * h_a + 2 * hp + 1)).astype(BF16)


def _prompt_attn(h_a, h_b, lam_init, batch, seq, tq, q, kt, v, lamv):
    nq = seq // tq
    n = batch * seq
    n_streams = 2 * h_a + h_b
    qspec = lambda w: pl.BlockSpec((tq, w), lambda b, i: (b * nq + i, 0))
    return pl.pallas_call(
        functools.partial(_prompt_attn_kernel, h_a, h_b, lam_init, tq),
        grid=(batch, nq),
        in_specs=[qspec(q.shape[1]),
                  pl.BlockSpec((None,) + kt.shape[1:], lambda b, i: (b, 0, 0, 0)),
                  pl.BlockSpec((seq, v.shape[1]), lambda b, i: (b, 0)),
                  _full(lamv)],
        out_specs=[qspec(h_a * LANES), qspec(h_b * 64)],
        out_shape=[jax.ShapeDtypeStruct((n, h_a * LANES), F32), jax.ShapeDtypeStruct((n, h_b * 64), BF16)],
        scratch_shapes=[pltpu.VMEM((n_streams, tq, LANES), F32)] * 3,
        compiler_params=_cparams(("parallel", "arbitrary")),
        name="prompt_attn",
    )(q, kt, v, lamv)


def _sample_attn_kernel(h_a, h_b, kv_lora, lam_init, npg, pt_ref, qa_ref, qlat_ref, qrope_ref, kn_ref, vn_ref,
                        cn_ref, rn_ref, lamv_ref, wuv_ref, *rest):
    kpg, vpg, cpg, rpg = (rest[i * npg:(i + 1) * npg] for i in range(4))
    oa_ref, ob_ref = rest[4 * npg:4 * npg + 2]
    qbd, ql, qr, m_a, l_a, acc_a, m_b, l_b, acc_b = rest[4 * npg + 2:]
    j = pl.program_id(1)
    t = qa_ref.shape[0]
    page = cpg[0].shape[0]
    rows_a = 2 * h_a * t
    rows_b = h_b * t

    def softmax_step(m_ref, l_ref, s):
        m_old = m_ref[...]
        m_new = jnp.maximum(m_old, jnp.max(s, axis=-1, keepdims=True))
        alpha = jnp.exp(m_old - m_new)
        p = jnp.exp(s - m_new)
        l_ref[...] = alpha * l_ref[...] + jnp.sum(p, axis=-1, keepdims=True)
        m_ref[...] = m_new
        return alpha, p

    def attend(sa, sb, n_chunks, val_a, val_b):
        alpha, p = softmax_step(m_a, l_a, sa)
        pv = []
        for h in range(h_a):
            ph = p[2 * t * h:2 * t * (h + 1)]
            acc = _dot(ph[:, 0:page].astype(BF16), val_a(h, 0))
            for i in range(1, n_chunks):
                acc = acc + _dot(ph[:, i * page:(i + 1) * page].astype(BF16), val_a(h, i))
            pv.append(acc)
        acc_a[...] = alpha * acc_a[...] + jnp.concatenate(pv, axis=0)
        alpha, p = softmax_step(m_b, l_b, sb)
        acc = _dot(p[:, 0:page].astype(BF16), val_b(0))
        for i in range(1, n_chunks):
            acc = acc + _dot(p[:, i * page:(i + 1) * page].astype(BF16), val_b(i))
        acc_b[...] = alpha * acc_b[...] + acc

    @pl.when(j == 0)
    def _():
        qa = qa_ref[...]
        r = lax.broadcasted_iota(jnp.int32, (rows_a, qa.shape[1]), 0)
        c = lax.broadcasted_iota(jnp.int32, (rows_a, qa.shape[1]), 1)
        dk = qa.shape[1] // (2 * h_a)
        qbd[...] = jnp.where((r >> _log2(t)) == (c >> _log2(dk)),
                             jnp.concatenate([qa] * (2 * h_a), axis=0), 0.0).astype(BF16)
        ql[...] = jnp.concatenate([qlat_ref[:, h * kv_lora:(h + 1) * kv_lora] for h in range(h_b)],
                                  axis=0).astype(BF16)
        qr[...] = jnp.concatenate([qrope_ref[:, h * 32:(h + 1) * 32] for h in range(h_b)], axis=0).astype(BF16)
        for m_ref, l_ref, acc_ref in ((m_a, l_a, acc_a), (m_b, l_b, acc_b)):
            m_ref[...] = jnp.full(m_ref.shape, NEG_INF, F32)
            l_ref[...] = jnp.zeros(l_ref.shape, F32)
            acc_ref[...] = jnp.zeros(acc_ref.shape, F32)
        pad = lambda ref: jnp.concatenate(
            [ref[...], jnp.zeros((page - t, ref.shape[1]), F32)], axis=0).astype(BF16)
        kn, vn, cn, rn = pad(kn_ref), pad(vn_ref), pad(cn_ref), pad(rn_ref)
        ca = lax.broadcasted_iota(jnp.int32, (rows_a, page), 1)
        ra = lax.broadcasted_iota(jnp.int32, (rows_a, page), 0)
        cb = lax.broadcasted_iota(jnp.int32, (rows_b, page), 1)
        rb = lax.broadcasted_iota(jnp.int32, (rows_b, page), 0)
        sa = jnp.where(ca <= (ra & (t - 1)), _dot_nt(qbd[...], kn), NEG_INF)
        sb = jnp.where(cb <= (rb & (t - 1)), _dot_nt(ql[...], cn) + _dot_nt(qr[...], rn), NEG_INF)
        attend(sa, sb, 1, lambda h, i: vn[:, _slab(h)], lambda i: cn)

    cs = [c[...].astype(BF16) for c in cpg]
    sa = jnp.concatenate([_dot(qbd[...], k[...].astype(BF16)) for k in kpg], axis=1)
    sb = jnp.concatenate([_dot_nt(ql[...], c) + _dot(qr[...], r[...].astype(BF16)) for c, r in zip(cs, rpg)],
                         axis=1)
    attend(sa, sb, npg, lambda h, i: vpg[i][pl.ds(h, page, stride=h_a), :].astype(BF16), lambda i: cs[i])

    @pl.when(j == pl.num_programs(1) - 1)
    def _():
        lam = _lambda(lamv_ref[...], lam_init)
        oa = acc_a[...] / l_a[...]
        for h in range(h_a):
            oa_ref[:, _slab(h)] = oa[(2 * h) * t:(2 * h + 1) * t] - lam * oa[(2 * h + 1) * t:(2 * h + 2) * t]
        olat = (acc_b[...] / l_b[...]).astype(BF16)
        ofull = _dot(olat, wuv_ref[...])
        cc = lax.broadcasted_iota(jnp.int32, (t, ofull.shape[1]), 1) >> 6
        ob = jnp.zeros((t, ofull.shape[1]), F32)
        for h in range(h_b):
            ob = jnp.where(cc == h, ofull[h * t:(h + 1) * t, :], ob)
        ob_ref[...] = ob


def _sample_attn(h_a, h_b, kv_lora, lam_init, npg, base, page_table, qa3, qlat3, qrope3, kn3, vn3, cn3, rn3,
                 lamv, wuv, cache_kt, cache_v, cache_c, cache_rt):
    s, t, n_a = qa3.shape
    n_pages = page_table.shape[1]
    steps = n_pages // npg
    seq = lambda a: pl.BlockSpec((None,) + a.shape[1:], lambda i, j, pt: (i, 0, 0))
    paged = lambda a, k: pl.BlockSpec((None,) + a.shape[1:],
                                      lambda i, j, pt, k=k: (base + pt[i, j * npg + k], 0, 0))
    caches = (cache_kt, cache_v, cache_c, cache_rt)
    in_specs = [seq(qa3), seq(qlat3), seq(qrope3), seq(kn3), seq(vn3), seq(cn3), seq(rn3), _full(lamv), _full(wuv)]
    in_specs += [paged(a, k) for a in caches for k in range(npg)]
    rows_a, rows_b = 2 * h_a * t, h_b * t
    dv_a = cache_v.shape[2]
    scratch = [pltpu.VMEM((rows_a, n_a), BF16), pltpu.VMEM((rows_b, kv_lora), BF16), pltpu.VMEM((rows_b, 32), BF16),
               pltpu.VMEM((rows_a, 1), F32), pltpu.VMEM((rows_a, 1), F32), pltpu.VMEM((rows_a, dv_a), F32),
               pltpu.VMEM((rows_b, 1), F32), pltpu.VMEM((rows_b, 1), F32), pltpu.VMEM((rows_b, kv_lora), F32)]
    out_wa, out_wb = h_a * dv_a, wuv.shape[1]
    grid_spec = pltpu.PrefetchScalarGridSpec(
        num_scalar_prefetch=1,
        grid=(s, steps),
        in_specs=in_specs,
        out_specs=[pl.BlockSpec((None, t, out_wa), lambda i, j, pt: (i, 0, 0)),
                   pl.BlockSpec((None, t, out_wb), lambda i, j, pt: (i, 0, 0))],
        scratch_shapes=scratch)
    args = [qa3, qlat3, qrope3, kn3, vn3, cn3, rn3, lamv, wuv]
    args += [a for a in caches for _ in range(npg)]
    return pl.pallas_call(
        functools.partial(_sample_attn_kernel, h_a, h_b, kv_lora, lam_init, npg),
        grid_spec=grid_spec,
        out_shape=[jax.ShapeDtypeStruct((s, t, out_wa), F32), jax.ShapeDtypeStruct((s, t, out_wb), F32)],
        compiler_params=_cparams(("parallel", "arbitrary")),
        name="sample_attn",
    )(page_table, *args)


def _mix_ffn_kernel(h_a, out_scale, ff_chunk, final, x_ref, mod_ref, oa_ref, ob_ref, gsub_ref, wout_ref,
                    w1_ref, w2_ref, gfin_ref, y_ref):
    x = x_ref[...]
    g, r, d = x.shape
    mod = mod_ref[...]
    gsub = gsub_ref[...] * out_scale
    parts = [(_rms(oa_ref[:, _slab(h)]) * gsub).astype(BF16) for h in range(h_a)]
    parts.append(ob_ref[...].astype(BF16))
    att = _dot(jnp.concatenate(parts, axis=1), wout_ref[...])
    x1 = x + mod[:, 2:3, :] * att.reshape(g, r, d)
    hm = (_rms(x1) * (1.0 + mod[:, 4:5, :]) + mod[:, 3:4, :]).reshape(g * r, d).astype(BF16)
    d_ff = w1_ref.shape[1]
    ff = jnp.zeros((g * r, d), F32)
    for c in range(d_ff // ff_chunk):
        a = _dot(hm, w1_ref[:, c * ff_chunk:(c + 1) * ff_chunk])
        a = jnp.square(jnp.maximum(a, 0.0)).astype(BF16)
        ff = ff + _dot(a, w2_ref[c * ff_chunk:(c + 1) * ff_chunk, :])
    x2 = x1 + mod[:, 5:6, :] * ff.reshape(g, r, d)
    if final:
        x2 = _rms(x2) * gfin_ref[...]
    y_ref[...] = x2


def _mix_ffn(h_a, out_scale, final, x3, mod3, oa, ob, gsub, wout, w1, w2, gfin, g_blk, tiles_per_mod):
    ng, r, d = x3.shape
    tm = g_blk * r
    row = lambda a: pl.BlockSpec((tm, a.shape[1]), lambda i: (i, 0))
    return pl.pallas_call(
        functools.partial(_mix_ffn_kernel, h_a, out_scale, 1024, final),
        grid=(ng // g_blk,),
        in_specs=[pl.BlockSpec((g_blk, r, d), lambda i: (i, 0, 0)),
                  pl.BlockSpec((g_blk, 6, d), lambda i: (i // tiles_per_mod, 0, 0)),
                  row(oa), row(ob), _full(gsub), _full(wout), _full(w1), _full(w2), _full(gfin)],
        out_specs=pl.BlockSpec((g_blk, r, d), lambda i: (i, 0, 0)),
        out_shape=jax.ShapeDtypeStruct((ng, r, d), F32),
        compiler_params=_cparams(("parallel",)),
        name="mix_ffn",
    )(x3, mod3, oa, ob, gsub, wout, w1, w2, gfin)


def _rope_angles(pos, d):
    inv = 1.0 / (ROPE_THETA ** (jnp.arange(0, d, 2, dtype=F32) / d))
    ang = pos.astype(F32)[:, None] * inv[None, :]
    return jnp.cos(ang), jnp.sin(ang)


def _rope_tables(pos, d_a, d_b):
    (ca, sa), (cb, sb) = _rope_angles(pos, d_a), _rope_angles(pos, d_b)
    n = pos.shape[0]
    reps = LANES // d_a
    cosa = jnp.tile(jnp.concatenate([ca, ca], axis=1), (1, reps))
    sina = jnp.tile(jnp.concatenate([-sa, sa], axis=1), (1, reps))
    one, zero = jnp.ones((n, 64), F32), jnp.zeros((n, 64), F32)
    cosb = jnp.concatenate([one, cb, cb, one[:, :LANES - 64 - d_b]], axis=1)
    sinb = jnp.concatenate([zero, -sb, sb, zero[:, :LANES - 64 - d_b]], axis=1)
    return cosa, sina, cosb, sinb


def kernel(x_prompt, x_sample, c_prompt, c_sample, cache_diff_k, cache_diff_v, cache_mla_ckv, cache_mla_krope,
           page_table, w_ada, b_ada, w_in, g_q, w_uq, w_uk, g_kv, lambda_q1, lambda_k1, lambda_q2, lambda_k2,
           g_sub, w_uv, w_out, w_ff1, w_ff2, g_final):
    batch, seq, d = x_prompt.shape
    s_batch, s_seq, _ = x_sample.shape
    depth, n_phys, page, h_a, _, dk_a = cache_diff_k.shape
    dv_a = cache_diff_v.shape[-1]
    kv_lora = cache_mla_ckv.shape[-1]
    qk_rope = cache_mla_krope.shape[-1]
    h_b, _, qk_nope = w_uk.shape[1:]
    v_head = w_uv.shape[-1]
    q_lora = g_q.shape[-1]
    past_len = page_table.shape[1] * page
    n_a = h_a * 2 * dk_a
    assert (dk_a, dv_a, qk_nope, qk_rope, v_head) == (64, 128, 64, 32, 64), "head geometry the lane layouts assume"
    assert w_in.shape[-1] == 3 * n_a + q_lora + kv_lora + qk_rope
    cfg = (h_a, dk_a, h_b, q_lora, kv_lora, dk_a ** -0.5, (qk_nope + qk_rope) ** -0.5)

    tm_p, ts, npg = 256, 32, 16
    pos_p = jnp.arange(seq)
    (ca_p, sa_p), (cb_p, sb_p) = _rope_angles(pos_p, dk_a), _rope_angles(pos_p, qk_rope)
    tab_p = _rope_tables(pos_p, dk_a, qk_rope) + (ca_p.T, sa_p.T, cb_p.T, sb_p.T)
    tab_s = tuple(jnp.tile(a, (ts, 1)) for a in _rope_tables(past_len + jnp.arange(s_seq), dk_a, qk_rope))

    xp3 = x_prompt.reshape(batch * seq // tm_p, tm_p, d)
    xs3 = x_sample
    c_all = jnp.concatenate([c_prompt, c_sample], axis=0)
    ckt = jnp.transpose(cache_diff_k, (0, 1, 3, 4, 5, 2)).reshape(depth * n_phys, n_a, page)
    cv = cache_diff_v.reshape(depth * n_phys, page * h_a, dv_a)
    cc = cache_mla_ckv.reshape(depth * n_phys, page, kv_lora)
    crt = jnp.transpose(cache_mla_krope, (0, 1, 3, 2)).reshape(depth * n_phys, qk_rope, page)

    sel_rows = (jnp.arange(h_b)[:, None] * LANES + 64 + jnp.arange(qk_rope)[None, :]).reshape(-1)
    sel = jnp.zeros((h_b * LANES, h_b * qk_rope), BF16).at[sel_rows, jnp.arange(h_b * qk_rope)].set(1.0)

    new_p, new_s = [], []
    for l in range(depth):
        lam_init = 0.8 - 0.6 * math.exp(-0.3 * l)
        mod = _adaln(c_all, w_ada[l], b_ada[l]).reshape(batch + s_batch, 6, d)
        mod_p, mod_s = mod[:batch], mod[batch:]
        off_ckv = 3 * n_a + q_lora
        off_kr = off_ckv + kv_lora
        w_l = w_in[l]
        win_s = jnp.concatenate([w_l[:, :off_kr], jnp.zeros((d, 64), F32), w_l[:, off_kr:],
                                 jnp.zeros((d, LANES - 64 - qk_rope), F32)], axis=1).astype(BF16)
        wmain = jnp.concatenate([w_l[:, :n_a], w_l[:, 2 * n_a:off_kr]], axis=1).astype(BF16)
        wkt = jnp.concatenate([w_l[:, n_a:2 * n_a], w_l[:, off_kr:]], axis=1).T.astype(BF16)
        wuq = jnp.pad(w_uq[l].reshape(q_lora, h_b, qk_nope + qk_rope),
                      ((0, 0), (0, 0), (0, LANES - qk_nope - qk_rope))).reshape(q_lora, h_b * LANES).astype(BF16)
        wukt = jnp.transpose(w_uk[l], (0, 2, 1)).reshape(h_b * qk_nope, kv_lora).astype(BF16)
        wuv_all = jnp.transpose(w_uv[l], (1, 0, 2)).reshape(kv_lora, h_b * v_head).astype(BF16)
        wabs = jnp.pad(jnp.transpose(w_uk[l], (0, 2, 1)), ((0, 0), (0, LANES - qk_nope), (0, 0))).astype(BF16)
        gq, gkv, gsub = g_q[l].reshape(1, -1), g_kv[l].reshape(1, -1), g_sub[l].reshape(1, -1)
        lamv = jnp.stack([lambda_q1[l], lambda_k1[l], lambda_q2[l], lambda_k2[l]], axis=0)
        wout, w1, w2 = w_out[l].astype(BF16), w_ff1[l].astype(BF16), w_ff2[l].astype(BF16)
        gfin = g_final.reshape(1, d)
        final = l == depth - 1

        q, kat, krt, kt, v, va, ckv = _proj_prompt(
            cfg, xp3, mod_p, tab_p, (wmain, wkt, gq, wuq, gkv, wukt, wuv_all), batch, seq)
        oa, ob = _prompt_attn(h_a, h_b, lam_init, batch, seq, tm_p, q, kt, v, lamv)
        xp3 = _mix_ffn(h_a, 1.0 - lam_init, final, xp3, mod_p, oa, ob, gsub, wout, w1, w2, gfin, 1, seq // tm_p)
        new_p.append((jnp.transpose(kat.reshape(batch, h_a, 2, dk_a, seq), (0, 4, 1, 2, 3)),
                      va.reshape(batch, seq, h_a, dv_a),
                      ckv.reshape(batch, seq, kv_lora),
                      jnp.transpose(krt, (0, 2, 1))))

        sqa, ska, sva, sckv, skr, sqlat, sqrope = _proj_sample(
            cfg, xs3, mod_s, tab_s, (win_s, gq, wuq, gkv, wabs, sel), ts)
        r3 = lambda a: a.reshape(s_batch, s_seq, a.shape[-1])
        soa, sob = _sample_attn(h_a, h_b, kv_lora, lam_init, npg, l * n_phys, page_table, r3(sqa), r3(sqlat),
                                r3(sqrope), r3(ska), r3(sva), r3(sckv), r3(skr), lamv, wuv_all, ckt, cv, cc, crt)
        n_s = s_batch * s_seq
        xs3 = _mix_ffn(h_a, 1.0 - lam_init, final, xs3, mod_s, soa.reshape(n_s, -1), sob.reshape(n_s, -1),
                       gsub, wout, w1, w2, gfin, ts, 1)
        new_s.append((ska.reshape(s_batch, s_seq, h_a, 2, dk_a), sva.reshape(s_batch, s_seq, h_a, dv_a),
                      sckv.reshape(s_batch, s_seq, kv_lora), skr.reshape(s_batch, s_seq, qk_rope)))

    y_prompt = xp3.reshape(batch, seq, d)
    y_sample = xs3
    stack = lambda rows, i: jnp.stack([r[i] for r in rows], axis=0)
    return (y_prompt, y_sample, stack(new_p, 0), stack(new_p, 1), stack(new_p, 2), stack(new_p, 3),
            stack(new_s, 0), stack(new_s, 1), stack(new_s, 2), stack(new_s, 3))
```

```python
import functools
import math

import jax
import jax.numpy as jnp
from jax import lax
from jax.experimental import pallas as pl
from jax.experimental.pallas import tpu as pltpu

F32 = jnp.float32
BF16 = jnp.bfloat16

ROPE_THETA = 10000.0
EPS = 1e-6
NEG_INF = -1e30
LOG2E = math.log2(math.e)
LANES = 128
VMEM_LIMIT = 56 * 1024 * 1024


def _cparams(sem):
    return pltpu.CompilerParams(dimension_semantics=sem, vmem_limit_bytes=VMEM_LIMIT)


def _rms(x):
    return x * lax.rsqrt(jnp.mean(x * x, axis=-1, keepdims=True) + EPS)


def _dot(a, b):
    return jnp.dot(a, b, preferred_element_type=F32)


def _dot_nt(a, b):
    return lax.dot_general(a, b, (((1,), (1,)), ((), ())), preferred_element_type=F32)


def _slab(i, w=LANES):
    return slice(i * w, (i + 1) * w)


def _log2(n):
    assert n & (n - 1) == 0, "power of two expected"
    return n.bit_length() - 1


def _rope(x, cos, sin_signed, half):
    lane = lax.broadcasted_iota(jnp.int32, x.shape, 1)
    first = (lane & (2 * half - 1)) < half
    swapped = jnp.where(first, pltpu.roll(x, LANES - half, 1), pltpu.roll(x, half, 1))
    return x * cos + swapped * sin_signed


def _rope_rows(x1, x2, cos, sin):
    return x1 * cos - x2 * sin, x2 * cos + x1 * sin


def _lambda(lamv, lam_init):
    e1 = jnp.exp(jnp.sum(lamv[0:1] * lamv[1:2], axis=-1, keepdims=True))
    e2 = jnp.exp(jnp.sum(lamv[2:3] * lamv[3:4], axis=-1, keepdims=True))
    return e1 - e2 + lam_init


def _full(a):
    return pl.BlockSpec(a.shape, lambda *idx, nd=a.ndim: (0,) * nd)


def _adaln_kernel(c_ref, w_ref, b_ref, o_ref):
    c = c_ref[...]
    s = c / (1.0 + jnp.exp(-c))
    o_ref[...] = _dot(s.astype(BF16), w_ref[...].astype(BF16)) + b_ref[...]


def _adaln(c_all, w_ada, b_ada):
    n, d = c_all.shape
    d_out = w_ada.shape[1]
    tn = 1024
    return pl.pallas_call(
        _adaln_kernel,
        grid=(d_out // tn,),
        in_specs=[pl.BlockSpec((n, d), lambda j: (0, 0)),
                  pl.BlockSpec((d, tn), lambda j: (0, j)),
                  pl.BlockSpec((1, tn), lambda j: (0, j))],
        out_specs=pl.BlockSpec((n, tn), lambda j: (0, j)),
        out_shape=jax.ShapeDtypeStruct((n, d_out), F32),
        compiler_params=_cparams(("arbitrary",)),
        name="adaln",
    )(c_all, w_ada, b_ada.reshape(1, d_out))


def _modulated(x_ref, mod_ref):
    x = x_ref[...]
    g, r, d = x.shape
    mod = mod_ref[...]
    h = _rms(x) * (1.0 + mod[:, 1:2, :]) + mod[:, 0:1, :]
    return h.reshape(g * r, d).astype(BF16)


def _proj_prompt_kernel(cfg, x_ref, mod_ref, cosa_ref, sina_ref, cosb_ref, sinb_ref, cosat_ref, sinat_ref,
                        cosbt_ref, sinbt_ref, wmain_ref, wkt_ref, gq_ref, wuq_ref, gkv_ref, wukt_ref, wuv_ref,
                        q_ref, kat_ref, krt_ref, kt_ref, v_ref, va_ref, ckv_ref):
    h_a, dk_a, h_b, q_lora, kv_lora, scale_a, scale_b = cfg
    scale_a, scale_b = scale_a * LOG2E, scale_b * LOG2E
    n_a = h_a * 2 * dk_a
    hmod = _modulated(x_ref, mod_ref)
    tm = hmod.shape[0]
    p = _dot(hmod, wmain_ref[...])
    kk = _dot_nt(wkt_ref[...], hmod)
    cosa, sina, cosb, sinb = cosa_ref[...], sina_ref[...], cosb_ref[...], sinb_ref[...]
    lane = lax.broadcasted_iota(jnp.int32, (tm, LANES), 1)
    low = lane < dk_a

    for h in range(h_a):
        q = _rope(p[:, _slab(h)], cosa, sina, dk_a // 2) * scale_a
        q_ref[:, _slab(2 * h)] = jnp.where(low, q, 0.0).astype(BF16)
        q_ref[:, _slab(2 * h + 1)] = jnp.where(low, 0.0, q).astype(BF16)
    cq = _rms(p[:, 2 * n_a:2 * n_a + q_lora]) * gq_ref[...]
    qb = _dot(cq.astype(BF16), wuq_ref[...])
    for h in range(h_b):
        q_ref[:, _slab(2 * h_a + h)] = (_rope(qb[:, _slab(h)], cosb, sinb, 16) * scale_b).astype(BF16)

    ca, sa = cosat_ref[...], sinat_ref[...]
    hd = dk_a // 2
    for g in range(n_a // dk_a):
        o1, o2 = _rope_rows(kk[g * dk_a:g * dk_a + hd], kk[g * dk_a + hd:(g + 1) * dk_a], ca, sa)
        kat_ref[g * dk_a:g * dk_a + hd, :] = o1
        kat_ref[g * dk_a + hd:(g + 1) * dk_a, :] = o2
        kt_ref[g * dk_a:g * dk_a + hd, :] = o1.astype(BF16)
        kt_ref[g * dk_a + hd:(g + 1) * dk_a, :] = o2.astype(BF16)
    r1, r2 = _rope_rows(kk[n_a:n_a + 16], kk[n_a + 16:n_a + 32], cosbt_ref[...], sinbt_ref[...])
    krt_ref[0:16, :] = r1
    krt_ref[16:32, :] = r2

    v = p[:, n_a:2 * n_a]
    for h in range(h_a):
        va_ref[pl.ds(h, tm, stride=h_a), :] = v[:, _slab(h)]
    v_ref[:, 0:n_a] = v.astype(BF16)
    ckv = _rms(p[:, 2 * n_a + q_lora:]) * gkv_ref[...]
    ckv_ref[...] = ckv
    ckv_bf = ckv.astype(BF16)
    v_ref[:, n_a:] = _dot(ckv_bf, wuv_ref[...]).astype(BF16)

    knt = _dot_nt(wukt_ref[...], ckv_bf)
    r1b, r2b = r1.astype(BF16), r2.astype(BF16)
    zero = jnp.zeros((32, tm), BF16)
    for h in range(h_b):
        base = n_a + h * LANES
        kt_ref[base:base + 64, :] = knt[h * 64:(h + 1) * 64].astype(BF16)
        kt_ref[base + 64:base + 80, :] = r1b
        kt_ref[base + 80:base + 96, :] = r2b
        kt_ref[base + 96:base + 128, :] = zero


def _proj_prompt(cfg, x3, mod3, tables, weights, batch, seq):
    ng, tm, d = x3.shape
    n = ng * tm
    tpb = seq // tm
    h_a, dk_a, h_b, q_lora, kv_lora, _, _ = cfg
    n_a = h_a * 2 * dk_a
    tab = pl.BlockSpec((tm, LANES), lambda i: (i % tpb, 0))
    tabt = lambda a: pl.BlockSpec((a.shape[0], tm), lambda i: (0, i % tpb))
    in_specs = [pl.BlockSpec((1, tm, d), lambda i: (i, 0, 0)),
                pl.BlockSpec((1, 6, d), lambda i: (i // tpb, 0, 0)),
                tab, tab, tab, tab] + [tabt(a) for a in tables[4:]] + [_full(w) for w in weights]
    row = lambda w: pl.BlockSpec((tm, w), lambda i: (i, 0))
    nq = (2 * h_a + h_b) * LANES
    nk = n_a + h_b * LANES
    nv = n_a + h_b * 64
    out_specs = [row(nq),
                 pl.BlockSpec((None, n_a, tm), lambda i: (i // tpb, 0, i % tpb)),
                 pl.BlockSpec((None, 32, tm), lambda i: (i // tpb, 0, i % tpb)),
                 pl.BlockSpec((None, None, nk, tm), lambda i: (i // tpb, i % tpb, 0, 0)),
                 row(nv),
                 pl.BlockSpec((tm * h_a, LANES), lambda i: (i, 0)),
                 row(kv_lora)]
    out_shape = [jax.ShapeDtypeStruct((n, nq), BF16),
                 jax.ShapeDtypeStruct((batch, n_a, seq), F32),
                 jax.ShapeDtypeStruct((batch, 32, seq), F32),
                 jax.ShapeDtypeStruct((batch, tpb, nk, tm), BF16),
                 jax.ShapeDtypeStruct((n, nv), BF16),
                 jax.ShapeDtypeStruct((n * h_a, LANES), F32),
                 jax.ShapeDtypeStruct((n, kv_lora), F32)]
    return pl.pallas_call(
        functools.partial(_proj_prompt_kernel, cfg),
        grid=(ng,),
        in_specs=in_specs,
        out_specs=out_specs,
        out_shape=out_shape,
        compiler_params=_cparams(("parallel",)),
        name="proj_prompt",
    )(x3, mod3, *tables, *weights)


def _proj_sample_kernel(cfg, x_ref, mod_ref, cosa_ref, sina_ref, cosb_ref, sinb_ref, win_ref, gq_ref, wuq_ref,
                        gkv_ref, wabs_ref, sel_ref, qa_ref, ka_ref, va_ref, ckv_ref, kr_ref, qlat_ref, qrope_ref):
    h_a, dk_a, h_b, q_lora, kv_lora, scale_a, scale_b = cfg
    n_a = h_a * 2 * dk_a
    off_cq = 3 * n_a
    off_ckv = off_cq + q_lora
    off_kr = off_ckv + kv_lora
    hmod = _modulated(x_ref, mod_ref)
    p = _dot(hmod, win_ref[...])
    cosa, sina, cosb, sinb = cosa_ref[...], sina_ref[...], cosb_ref[...], sinb_ref[...]
    for s in range(n_a // LANES):
        qa_ref[:, _slab(s)] = _rope(p[:, _slab(s)], cosa, sina, dk_a // 2) * scale_a
        ka_ref[:, _slab(s)] = _rope(p[:, n_a + s * LANES:n_a + (s + 1) * LANES], cosa, sina, dk_a // 2)
    va_ref[...] = p[:, 2 * n_a:3 * n_a]
    cq = _rms(p[:, off_cq:off_ckv]) * gq_ref[...]
    qb = _dot(cq.astype(BF16), wuq_ref[...])
    ckv_ref[...] = _rms(p[:, off_ckv:off_kr]) * gkv_ref[...]
    kr_ref[...] = _rope(p[:, off_kr:off_kr + LANES], cosb, sinb, 16)[:, 64:96]
    qbr = [(_rope(qb[:, _slab(h)], cosb, sinb, 16) * scale_b).astype(BF16) for h in range(h_b)]
    for h in range(h_b):
        qlat_ref[:, h * kv_lora:(h + 1) * kv_lora] = _dot(qbr[h], wabs_ref[h])
    qrope_ref[...] = _dot(jnp.concatenate(qbr, axis=1), sel_ref[...])


def _proj_sample(cfg, x3, mod3, tables, weights, g_blk):
    ng, r, d = x3.shape
    tm = g_blk * r
    n = ng * r
    h_a, dk_a, h_b, q_lora, kv_lora, _, _ = cfg
    n_a = h_a * 2 * dk_a
    tab = pl.BlockSpec((tm, LANES), lambda i: (0, 0))
    in_specs = [pl.BlockSpec((g_blk, r, d), lambda i: (i, 0, 0)),
                pl.BlockSpec((g_blk, 6, d), lambda i: (i, 0, 0)),
                tab, tab, tab, tab] + [_full(w) for w in weights]
    widths = [n_a, n_a, n_a, kv_lora, 32, h_b * kv_lora, h_b * 32]
    return pl.pallas_call(
        functools.partial(_proj_sample_kernel, cfg),
        grid=(ng // g_blk,),
        in_specs=in_specs,
        out_specs=[pl.BlockSpec((tm, w), lambda i: (i, 0)) for w in widths],
        out_shape=[jax.ShapeDtypeStruct((n, w), F32) for w in widths],
        compiler_params=_cparams(("parallel",)),
        name="proj_sample",
    )(x3, mod3, *tables, *weights)


def _prompt_attn_kernel(h_a, h_b, lam_init, tq, q_ref, kt_ref, v_ref, lamv_ref, oa_ref, ob_ref,
                        m_ref, l_ref, acc_ref):
    qi = pl.program_id(1)
    n_streams = 2 * h_a + h_b
    row = lax.broadcasted_iota(jnp.int32, (tq, tq), 0)
    col = lax.broadcasted_iota(jnp.int32, (tq, tq), 1)
    causal = col <= row

    def slabs(st):
        if st < 2 * h_a:
            return st // 2, st // 2
        h = st - 2 * h_a
        return h_a + h, h_a + h // 2

    m_ref[...] = jnp.full(m_ref.shape, NEG_INF, F32)
    l_ref[...] = jnp.zeros(l_ref.shape, F32)
    acc_ref[...] = jnp.zeros(acc_ref.shape, F32)

    def block(js, masked):
        for st in range(n_streams):
            ks, vs = slabs(st)
            q = q_ref[:, _slab(st)]
            ss = [_dot(q, kt_ref[j, _slab(ks), :]) for j in js]
            if masked:
                ss[-1] = jnp.where(causal, ss[-1], NEG_INF)
            s = jnp.concatenate(ss, axis=1)
            n_tiles = s.shape[1] // LANES
            m_old = m_ref[st]
            m_new = jnp.maximum(m_old, jnp.max(s, axis=-1, keepdims=True))
            alpha = jnp.exp2(m_old - m_new)
            p = jnp.exp2(s - jnp.concatenate([m_new] * n_tiles, axis=1))
            psum = p[:, 0:LANES]
            for c in range(1, n_tiles):
                psum = psum + p[:, _slab(c)]
            l_ref[st] = alpha * l_ref[st] + psum
            m_ref[st] = m_new
            pv = None
            for i, j in enumerate(js):
                k0 = pl.multiple_of(j * tq, tq)
                d = _dot(p[:, i * tq:(i + 1) * tq].astype(BF16), v_ref[pl.ds(k0, tq), _slab(vs)])
                pv = d if pv is None else pv + d
            acc_ref[st] = alpha * acc_ref[st] + pv

    def body(jj, carry):
        block([2 * jj], False)
        block([2 * jj + 1], False)
        return carry
    lax.fori_loop(0, qi // 2, body, 0)

    @pl.when(qi % 2 == 1)
    def _():
        block([qi - 1], False)
    block([qi], True)

    lam = _lambda(lamv_ref[...], lam_init)
    out = lambda st: acc_ref[st] / jnp.sum(l_ref[st], axis=-1, keepdims=True)
    for h in range(h_a):
        oa_ref[:, _slab(h)] = out(2 * h) - lam * out(2 * h + 1)
    low = lax.broadcasted_iota(jnp.int32, (tq, LANES), 1) < 64
    for hp in range(h_b // 2):
        ob_ref[:, _slab(hp)] = jnp.where(low, out(2 * h_a + 2 * hp), out(2 * h_a + 2 * hp + 1)).astype(BF16)


def _prompt_attn(h_a, h_b, lam_init, batch, seq, tq, q, kt, v, lamv):
    nq = seq // tq
    n = batch * seq
    n_streams = 2 * h_a + h_b
    qspec = lambda w: pl.BlockSpec((tq, w), lambda b, i: (b * nq + i, 0))
    return pl.pallas_call(
        functools.partial(_prompt_attn_kernel, h_a, h_b, lam_init, tq),
        grid=(batch, nq),
        in_specs=[qspec(q.shape[1]),
                  pl.BlockSpec((None,) + kt.shape[1:], lambda b, i: (b, 0, 0, 0)),
                  pl.BlockSpec((seq, v.shape[1]), lambda b, i: (b, 0)),
                  _full(lamv)],
        out_specs=[qspec(h_a * LANES), qspec(h_b * 64)],
        out_shape=[jax.ShapeDtypeStruct((n, h_a * LANES), F32), jax.ShapeDtypeStruct((n, h_b * 64), BF16)],
        scratch_shapes=[pltpu.VMEM((n_streams, tq, LANES), F32)] * 3,
        compiler_params=_cparams(("parallel", "arbitrary")),
        name="prompt_attn",
    )(q, kt, v, lamv)


def _sample_attn_kernel(h_a, h_b, kv_lora, lam_init, npg, base, pt_ref, qa_ref, qlat_ref, qrope_ref, kn_ref,
                        vn_ref, cn_ref, rn_ref, lamv_ref, wuv_ref, ck_hbm, cv_hbm, cc_hbm, cr_hbm, oa_ref, ob_ref,
                        qbd, ql, qr, m_a, l_a, acc_a, m_b, l_b, acc_b, kbuf, vbuf, cbuf, rbuf, sem):
    i, j = pl.program_id(0), pl.program_id(1)
    steps = pl.num_programs(1)
    total = pl.num_programs(0) * steps
    n = i * steps + j
    slot = n % 2
    t = qa_ref.shape[0]
    page = cbuf.shape[2]
    rows_a = 2 * h_a * t
    rows_b = h_b * t
    caches, bufs = (ck_hbm, cv_hbm, cc_hbm, cr_hbm), (kbuf, vbuf, cbuf, rbuf)

    def page_copies(step, dst_slot):
        si, sj = step // steps, step % steps
        copies = []
        for k in range(npg):
            pid = base + pt_ref[si, sj * npg + k]
            for kind in range(4):
                copies.append(pltpu.make_async_copy(caches[kind].at[pid], bufs[kind].at[dst_slot, k],
                                                    sem.at[dst_slot, kind]))
        return copies

    @pl.when(n == 0)
    def _():
        for c in page_copies(0, 0):
            c.start()

        @pl.when(total > 1)
        def _():
            for c in page_copies(1, 1):
                c.start()

    def softmax_step(m_ref, l_ref, s):
        m_old = m_ref[...]
        m_new = jnp.maximum(m_old, jnp.max(s, axis=-1, keepdims=True))
        alpha = jnp.exp(m_old - m_new)
        p = jnp.exp(s - m_new)
        l_ref[...] = alpha * l_ref[...] + jnp.sum(p, axis=-1, keepdims=True)
        m_ref[...] = m_new
        return alpha, p

    def attend(sa, sb, n_chunks, val_a, val_b):
        alpha, p = softmax_step(m_a, l_a, sa)
        pv = []
        for h in range(h_a):
            ph = p[2 * t * h:2 * t * (h + 1)]
            acc = _dot(ph[:, 0:page].astype(BF16), val_a(h, 0))
            for i in range(1, n_chunks):
                acc = acc + _dot(ph[:, i * page:(i + 1) * page].astype(BF16), val_a(h, i))
            pv.append(acc)
        acc_a[...] = alpha * acc_a[...] + jnp.concatenate(pv, axis=0)
        alpha, p = softmax_step(m_b, l_b, sb)
        acc = _dot(p[:, 0:page].astype(BF16), val_b(0))
        for i in range(1, n_chunks):
            acc = acc + _dot(p[:, i * page:(i + 1) * page].astype(BF16), val_b(i))
        acc_b[...] = alpha * acc_b[...] + acc

    @pl.when(j == 0)
    def _():
        qa = qa_ref[...]
        r = lax.broadcasted_iota(jnp.int32, (rows_a, qa.shape[1]), 0)
        c = lax.broadcasted_iota(jnp.int32, (rows_a, qa.shape[1]), 1)
        dk = qa.shape[1] // (2 * h_a)
        qbd[...] = jnp.where((r >> _log2(t)) == (c >> _log2(dk)),
                             jnp.concatenate([qa] * (2 * h_a), axis=0), 0.0).astype(BF16)
        ql[...] = jnp.concatenate([qlat_ref[:, h * kv_lora:(h + 1) * kv_lora] for h in range(h_b)],
                                  axis=0).astype(BF16)
        qr[...] = jnp.concatenate([qrope_ref[:, h * 32:(h + 1) * 32] for h in range(h_b)], axis=0).astype(BF16)
        for m_ref, l_ref, acc_ref in ((m_a, l_a, acc_a), (m_b, l_b, acc_b)):
            m_ref[...] = jnp.full(m_ref.shape, NEG_INF, F32)
            l_ref[...] = jnp.zeros(l_ref.shape, F32)
            acc_ref[...] = jnp.zeros(acc_ref.shape, F32)
        pad = lambda ref: jnp.concatenate(
            [ref[...], jnp.zeros((page - t, ref.shape[1]), F32)], axis=0).astype(BF16)
        kn, vn, cn, rn = pad(kn_ref), pad(vn_ref), pad(cn_ref), pad(rn_ref)
        ca = lax.broadcasted_iota(jnp.int32, (rows_a, page), 1)
        ra = lax.broadcasted_iota(jnp.int32, (rows_a, page), 0)
        cb = lax.broadcasted_iota(jnp.int32, (rows_b, page), 1)
        rb = lax.broadcasted_iota(jnp.int32, (rows_b, page), 0)
        sa = jnp.where(ca <= (ra & (t - 1)), _dot_nt(qbd[...], kn), NEG_INF)
        sb = jnp.where(cb <= (rb & (t - 1)), _dot_nt(ql[...], cn) + _dot_nt(qr[...], rn), NEG_INF)
        attend(sa, sb, 1, lambda h, i: vn[:, _slab(h)], lambda i: cn)

    for c in page_copies(n, slot):
        c.wait()
    cs = [cbuf[slot, k].astype(BF16) for k in range(npg)]
    sa = jnp.concatenate([_dot(qbd[...], kbuf[slot, k].astype(BF16)) for k in range(npg)], axis=1)
    sb = jnp.concatenate([_dot_nt(ql[...], cs[k]) + _dot(qr[...], rbuf[slot, k].astype(BF16))
                          for k in range(npg)], axis=1)
    attend(sa, sb, npg, lambda h, k: vbuf[slot, k, pl.ds(h, page, stride=h_a), :].astype(BF16), lambda k: cs[k])

    @pl.when(j == pl.num_programs(1) - 1)
    def _():
        lam = _lambda(lamv_ref[...], lam_init)
        oa = acc_a[...] / l_a[...]
        for h in range(h_a):
            oa_ref[:, _slab(h)] = oa[(2 * h) * t:(2 * h + 1) * t] - lam * oa[(2 * h + 1) * t:(2 * h + 2) * t]
        olat = (acc_b[...] / l_b[...]).astype(BF16)
        ofull = _dot(olat, wuv_ref[...])
        cc = lax.broadcasted_iota(jnp.int32, (t, ofull.shape[1]), 1) >> 6
        ob = jnp.zeros((t, ofull.shape[1]), F32)
        for h in range(h_b):
            ob = jnp.where(cc == h, ofull[h * t:(h + 1) * t, :], ob)
        ob_ref[...] = ob

    @pl.when(n + 2 < total)
    def _():
        for c in page_copies(n + 2, slot):
            c.start()


def _sample_attn(h_a, h_b, kv_lora, lam_init, npg, base, page_table, qa3, qlat3, qrope3, kn3, vn3, cn3, rn3,
                 lamv, wuv, cache_kt, cache_v, cache_c, cache_rt):
    s, t, n_a = qa3.shape
    n_pages = page_table.shape[1]
    steps = n_pages // npg
    seq = lambda a: pl.BlockSpec((None,) + a.shape[1:], lambda i, j, pt: (i, 0, 0))
    caches = (cache_kt, cache_v, cache_c, cache_rt)
    in_specs = [seq(qa3), seq(qlat3), seq(qrope3), seq(kn3), seq(vn3), seq(cn3), seq(rn3), _full(lamv), _full(wuv)]
    in_specs += [pl.BlockSpec(memory_space=pl.ANY)] * len(caches)
    rows_a, rows_b = 2 * h_a * t, h_b * t
    dv_a = cache_v.shape[2]
    scratch = [pltpu.VMEM((rows_a, n_a), BF16), pltpu.VMEM((rows_b, kv_lora), BF16), pltpu.VMEM((rows_b, 32), BF16),
               pltpu.VMEM((rows_a, 1), F32), pltpu.VMEM((rows_a, 1), F32), pltpu.VMEM((rows_a, dv_a), F32),
               pltpu.VMEM((rows_b, 1), F32), pltpu.VMEM((rows_b, 1), F32), pltpu.VMEM((rows_b, kv_lora), F32)]
    scratch += [pltpu.VMEM((2, npg) + a.shape[1:], F32) for a in caches]
    scratch += [pltpu.SemaphoreType.DMA((2, len(caches)))]
    out_wa, out_wb = h_a * dv_a, wuv.shape[1]
    grid_spec = pltpu.PrefetchScalarGridSpec(
        num_scalar_prefetch=1,
        grid=(s, steps),
        in_specs=in_specs,
        out_specs=[pl.BlockSpec((None, t, out_wa), lambda i, j, pt: (i, 0, 0)),
                   pl.BlockSpec((None, t, out_wb), lambda i, j, pt: (i, 0, 0))],
        scratch_shapes=scratch)
    return pl.pallas_call(
        functools.partial(_sample_attn_kernel, h_a, h_b, kv_lora, lam_init, npg, base),
        grid_spec=grid_spec,
        out_shape=[jax.ShapeDtypeStruct((s, t, out_wa), F32), jax.ShapeDtypeStruct((s, t, out_wb), F32)],
        compiler_params=_cparams(("arbitrary", "arbitrary")),
        name="sample_attn",
    )(page_table, qa3, qlat3, qrope3, kn3, vn3, cn3, rn3, lamv, wuv, *caches)


def _mix_ffn_kernel(h_a, out_scale, ff_chunk, final, x_ref, mod_ref, oa_ref, ob_ref, gsub_ref, wout_ref,
                    w1_ref, w2_ref, gfin_ref, y_ref):
    x = x_ref[...]
    g, r, d = x.shape
    mod = mod_ref[...]
    gsub = gsub_ref[...] * out_scale
    parts = [(_rms(oa_ref[:, _slab(h)]) * gsub).astype(BF16) for h in range(h_a)]
    parts.append(ob_ref[...].astype(BF16))
    att = _dot(jnp.concatenate(parts, axis=1), wout_ref[...])
    x1 = x + mod[:, 2:3, :] * att.reshape(g, r, d)
    hm = (_rms(x1) * (1.0 + mod[:, 4:5, :]) + mod[:, 3:4, :]).reshape(g * r, d).astype(BF16)
    d_ff = w1_ref.shape[1]
    ff = jnp.zeros((g * r, d), F32)
    for c in range(d_ff // ff_chunk):
        a = _dot(hm, w1_ref[:, c * ff_chunk:(c + 1) * ff_chunk])
        a = jnp.square(jnp.maximum(a, 0.0)).astype(BF16)
        ff = ff + _dot(a, w2_ref[c * ff_chunk:(c + 1) * ff_chunk, :])
    x2 = x1 + mod[:, 5:6, :] * ff.reshape(g, r, d)
    if final:
        x2 = _rms(x2) * gfin_ref[...]
    y_ref[...] = x2


def _mix_ffn(h_a, out_scale, final, x3, mod3, oa, ob, gsub, wout, w1, w2, gfin, g_blk, tiles_per_mod):
    ng, r, d = x3.shape
    tm = g_blk * r
    row = lambda a: pl.BlockSpec((tm, a.shape[1]), lambda i: (i, 0))
    return pl.pallas_call(
        functools.partial(_mix_ffn_kernel, h_a, out_scale, 1024, final),
        grid=(ng // g_blk,),
        in_specs=[pl.BlockSpec((g_blk, r, d), lambda i: (i, 0, 0)),
                  pl.BlockSpec((g_blk, 6, d), lambda i: (i // tiles_per_mod, 0, 0)),
                  row(oa), row(ob), _full(gsub), _full(wout), _full(w1), _full(w2), _full(gfin)],
        out_specs=pl.BlockSpec((g_blk, r, d), lambda i: (i, 0, 0)),
        out_shape=jax.ShapeDtypeStruct((ng, r, d), F32),
        compiler_params=_cparams(("parallel",)),
        name="mix_ffn",
    )(x3, mod3, oa, ob, gsub, wout, w1, w2, gfin)


def _rope_angles(pos, d):
    inv = 1.0 / (ROPE_THETA ** (jnp.arange(0, d, 2, dtype=F32) / d))
    ang = pos.astype(F32)[:, None] * inv[None, :]
    return jnp.cos(ang), jnp.sin(ang)


def _rope_tables(pos, d_a, d_b):
    (ca, sa), (cb, sb) = _rope_angles(pos, d_a), _rope_angles(pos, d_b)
    n = pos.shape[0]
    reps = LANES // d_a
    cosa = jnp.tile(jnp.concatenate([ca, ca], axis=1), (1, reps))
    sina = jnp.tile(jnp.concatenate([-sa, sa], axis=1), (1, reps))
    one, zero = jnp.ones((n, 64), F32), jnp.zeros((n, 64), F32)
    cosb = jnp.concatenate([one, cb, cb, one[:, :LANES - 64 - d_b]], axis=1)
    sinb = jnp.concatenate([zero, -sb, sb, zero[:, :LANES - 64 - d_b]], axis=1)
    return cosa, sina, cosb, sinb


def kernel(x_prompt, x_sample, c_prompt, c_sample, cache_diff_k, cache_diff_v, cache_mla_ckv, cache_mla_krope,
           page_table, w_ada, b_ada, w_in, g_q, w_uq, w_uk, g_kv, lambda_q1, lambda_k1, lambda_q2, lambda_k2,
           g_sub, w_uv, w_out, w_ff1, w_ff2, g_final):
    batch, seq, d = x_prompt.shape
    s_batch, s_seq, _ = x_sample.shape
    depth, n_phys, page, h_a, _, dk_a = cache_diff_k.shape
    dv_a = cache_diff_v.shape[-1]
    kv_lora = cache_mla_ckv.shape[-1]
    qk_rope = cache_mla_krope.shape[-1]
    h_b, _, qk_nope = w_uk.shape[1:]
    v_head = w_uv.shape[-1]
    q_lora = g_q.shape[-1]
    past_len = page_table.shape[1] * page
    n_a = h_a * 2 * dk_a
    assert (dk_a, dv_a, qk_nope, qk_rope, v_head) == (64, 128, 64, 32, 64), "head geometry the lane layouts assume"
    assert w_in.shape[-1] == 3 * n_a + q_lora + kv_lora + qk_rope
    cfg = (h_a, dk_a, h_b, q_lora, kv_lora, dk_a ** -0.5, (qk_nope + qk_rope) ** -0.5)

    tm_p, ts, npg = 256, 32, 16
    pos_p = jnp.arange(seq)
    (ca_p, sa_p), (cb_p, sb_p) = _rope_angles(pos_p, dk_a), _rope_angles(pos_p, qk_rope)
    tab_p = _rope_tables(pos_p, dk_a, qk_rope) + (ca_p.T, sa_p.T, cb_p.T, sb_p.T)
    tab_s = tuple(jnp.tile(a, (ts, 1)) for a in _rope_tables(past_len + jnp.arange(s_seq), dk_a, qk_rope))

    xp3 = x_prompt.reshape(batch * seq // tm_p, tm_p, d)
    xs3 = x_sample
    c_all = jnp.concatenate([c_prompt, c_sample], axis=0)
    ckt = jnp.transpose(cache_diff_k, (0, 1, 3, 4, 5, 2)).reshape(depth * n_phys, n_a, page)
    cv = cache_diff_v.reshape(depth * n_phys, page * h_a, dv_a)
    cc = cache_mla_ckv.reshape(depth * n_phys, page, kv_lora)
    crt = jnp.transpose(cache_mla_krope, (0, 1, 3, 2)).reshape(depth * n_phys, qk_rope, page)

    sel_rows = (jnp.arange(h_b)[:, None] * LANES + 64 + jnp.arange(qk_rope)[None, :]).reshape(-1)
    sel = jnp.zeros((h_b * LANES, h_b * qk_rope), BF16).at[sel_rows, jnp.arange(h_b * qk_rope)].set(1.0)

    new_p, new_s = [], []
    for l in range(depth):
        lam_init = 0.8 - 0.6 * math.exp(-0.3 * l)
        mod = _adaln(c_all, w_ada[l], b_ada[l]).reshape(batch + s_batch, 6, d)
        mod_p, mod_s = mod[:batch], mod[batch:]
        off_ckv = 3 * n_a + q_lora
        off_kr = off_ckv + kv_lora
        w_l = w_in[l]
        win_s = jnp.concatenate([w_l[:, :off_kr], jnp.zeros((d, 64), F32), w_l[:, off_kr:],
                                 jnp.zeros((d, LANES - 64 - qk_rope), F32)], axis=1).astype(BF16)
        wmain = jnp.concatenate([w_l[:, :n_a], w_l[:, 2 * n_a:off_kr]], axis=1).astype(BF16)
        wkt = jnp.concatenate([w_l[:, n_a:2 * n_a], w_l[:, off_kr:]], axis=1).T.astype(BF16)
        wuq = jnp.pad(w_uq[l].reshape(q_lora, h_b, qk_nope + qk_rope),
                      ((0, 0), (0, 0), (0, LANES - qk_nope - qk_rope))).reshape(q_lora, h_b * LANES).astype(BF16)
        wukt = jnp.transpose(w_uk[l], (0, 2, 1)).reshape(h_b * qk_nope, kv_lora).astype(BF16)
        wuv_all = jnp.transpose(w_uv[l], (1, 0, 2)).reshape(kv_lora, h_b * v_head).astype(BF16)
        wabs = jnp.pad(jnp.transpose(w_uk[l], (0, 2, 1)), ((0, 0), (0, LANES - qk_nope), (0, 0))).astype(BF16)
        gq, gkv, gsub = g_q[l].reshape(1, -1), g_kv[l].reshape(1, -1), g_sub[l].reshape(1, -1)
        lamv = jnp.stack([lambda_q1[l], lambda_k1[l], lambda_q2[l], lambda_k2[l]], axis=0)
        wout, w1, w2 = w_out[l].astype(BF16), w_ff1[l].astype(BF16), w_ff2[l].astype(BF16)
        gfin = g_final.reshape(1, d)
        final = l == depth - 1

        q, kat, krt, kt, v, va, ckv = _proj_prompt(
            cfg, xp3, mod_p, tab_p, (wmain, wkt, gq, wuq, gkv, wukt, wuv_all), batch, seq)
        oa, ob = _prompt_attn(h_a, h_b, lam_init, batch, seq, tm_p, q, kt, v, lamv)
        xp3 = _mix_ffn(h_a, 1.0 - lam_init, final, xp3, mod_p, oa, ob, gsub, wout, w1, w2, gfin, 1, seq // tm_p)
        new_p.append((jnp.transpose(kat.reshape(batch, h_a, 2, dk_a, seq), (0, 4, 1, 2, 3)),
                      va.reshape(batch, seq, h_a, dv_a),
                      ckv.reshape(batch, seq, kv_lora),
                      jnp.transpose(krt, (0, 2, 1))))

        sqa, ska, sva, sckv, skr, sqlat, sqrope = _proj_sample(
            cfg, xs3, mod_s, tab_s, (win_s, gq, wuq, gkv, wabs, sel), ts)
        r3 = lambda a: a.reshape(s_batch, s_seq, a.shape[-1])
        soa, sob = _sample_attn(h_a, h_b, kv_lora, lam_init, npg, l * n_phys, page_table, r3(sqa), r3(sqlat),
                                r3(sqrope), r3(ska), r3(sva), r3(sckv), r3(skr), lamv, wuv_all, ckt, cv, cc, crt)
        n_s = s_batch * s_seq
        xs3 = _mix_ffn(h_a, 1.0 - lam_init, final, xs3, mod_s, soa.reshape(n_s, -1), sob.reshape(n_s, -1),
                       gsub, wout, w1, w2, gfin, ts, 1)
        new_s.append((ska.reshape(s_batch, s_seq, h_a, 2, dk_a), sva.reshape(s_batch, s_seq, h_a, dv_a),
                      sckv.reshape(s_batch, s_seq, kv_lora), skr.reshape(s_batch, s_seq, qk_rope)))

    y_prompt = xp3.reshape(batch, seq, d)
    y_sample = xs3
    stack = lambda rows, i: jnp.stack([r[i] for r in rows], axis=0)
    return (y_prompt, y_sample, stack(new_p, 0), stack(new_p, 1), stack(new_p, 2), stack(new_p, 3),
            stack(new_s, 0), stack(new_s, 1), stack(new_s, 2), stack(new_s, 3))
```

```python
import functools
import math

import jax
import jax.numpy as jnp
from jax import lax
from jax.experimental import pallas as pl
from jax.experimental.pallas import tpu as pltpu

F32 = jnp.float32
BF16 = jnp.bfloat16

ROPE_THETA = 10000.0
EPS = 1e-6
NEG_INF = -1e30
LOG2E = math.log2(math.e)
LANES = 128
VMEM_LIMIT = 56 * 1024 * 1024


def _cparams(sem):
    return pltpu.CompilerParams(dimension_semantics=sem, vmem_limit_bytes=VMEM_LIMIT)


def _rms(x):
    return x * lax.rsqrt(jnp.mean(x * x, axis=-1, keepdims=True) + EPS)


def _dot(a, b):
    return jnp.dot(a, b, preferred_element_type=F32)


def _dot_nt(a, b):
    return lax.dot_general(a, b, (((1,), (1,)), ((), ())), preferred_element_type=F32)


def _slab(i, w=LANES):
    return slice(i * w, (i + 1) * w)


def _log2(n):
    assert n & (n - 1) == 0, "power of two expected"
    return n.bit_length() - 1


def _rope(x, cos, sin_signed, half):
    lane = lax.broadcasted_iota(jnp.int32, x.shape, 1)
    first = (lane & (2 * half - 1)) < half
    swapped = jnp.where(first, pltpu.roll(x, LANES - half, 1), pltpu.roll(x, half, 1))
    return x * cos + swapped * sin_signed


def _rope_rows(x1, x2, cos, sin):
    return x1 * cos - x2 * sin, x2 * cos + x1 * sin


def _lambda(lamv, lam_init):
    e1 = jnp.exp(jnp.sum(lamv[0:1] * lamv[1:2], axis=-1, keepdims=True))
    e2 = jnp.exp(jnp.sum(lamv[2:3] * lamv[3:4], axis=-1, keepdims=True))
    return e1 - e2 + lam_init


def _full(a):
    return pl.BlockSpec(a.shape, lambda *idx, nd=a.ndim: (0,) * nd)


def _adaln_kernel(c_ref, w_ref, b_ref, o_ref):
    c = c_ref[...]
    s = c / (1.0 + jnp.exp(-c))
    o_ref[...] = _dot(s.astype(BF16), w_ref[...].astype(BF16)) + b_ref[...]


def _adaln(c_all, w_ada, b_ada):
    n, d = c_all.shape
    d_out = w_ada.shape[1]
    tn = 1024
    return pl.pallas_call(
        _adaln_kernel,
        grid=(d_out // tn,),
        in_specs=[pl.BlockSpec((n, d), lambda j: (0, 0)),
                  pl.BlockSpec((d, tn), lambda j: (0, j)),
                  pl.BlockSpec((1, tn), lambda j: (0, j))],
        out_specs=pl.BlockSpec((n, tn), lambda j: (0, j)),
        out_shape=jax.ShapeDtypeStruct((n, d_out), F32),
        compiler_params=_cparams(("arbitrary",)),
        name="adaln",
    )(c_all, w_ada, b_ada.reshape(1, d_out))


def _modulated(x_ref, mod_ref):
    x = x_ref[...]
    g, r, d = x.shape
    mod = mod_ref[...]
    h = _rms(x) * (1.0 + mod[:, 1:2, :]) + mod[:, 0:1, :]
    return h.reshape(g * r, d).astype(BF16)


def _proj_prompt_kernel(cfg, x_ref, mod_ref, cosa_ref, sina_ref, cosb_ref, sinb_ref, cosat_ref, sinat_ref,
                        cosbt_ref, sinbt_ref, wmain_ref, wkt_ref, gq_ref, wuq_ref, gkv_ref, wukt_ref, wuv_ref,
                        q_ref, kat_ref, krt_ref, kt_ref, v_ref, va_ref, ckv_ref):
    h_a, dk_a, h_b, q_lora, kv_lora, scale_a, scale_b = cfg
    scale_a, scale_b = scale_a * LOG2E, scale_b * LOG2E
    n_a = h_a * 2 * dk_a
    hmod = _modulated(x_ref, mod_ref)
    tm = hmod.shape[0]
    p = _dot(hmod, wmain_ref[...])
    kk = _dot_nt(wkt_ref[...], hmod)
    cosa, sina, cosb, sinb = cosa_ref[...], sina_ref[...], cosb_ref[...], sinb_ref[...]
    lane = lax.broadcasted_iota(jnp.int32, (tm, LANES), 1)
    low = lane < dk_a

    for h in range(h_a):
        q = _rope(p[:, _slab(h)], cosa, sina, dk_a // 2) * scale_a
        q_ref[:, _slab(2 * h)] = jnp.where(low, q, 0.0).astype(BF16)
        q_ref[:, _slab(2 * h + 1)] = jnp.where(low, 0.0, q).astype(BF16)
    cq = _rms(p[:, 2 * n_a:2 * n_a + q_lora]) * gq_ref[...]
    qb = _dot(cq.astype(BF16), wuq_ref[...])
    for h in range(h_b):
        q_ref[:, _slab(2 * h_a + h)] = (_rope(qb[:, _slab(h)], cosb, sinb, 16) * scale_b).astype(BF16)

    ca, sa = cosat_ref[...], sinat_ref[...]
    hd = dk_a // 2
    for g in range(n_a // dk_a):
        o1, o2 = _rope_rows(kk[g * dk_a:g * dk_a + hd], kk[g * dk_a + hd:(g + 1) * dk_a], ca, sa)
        kat_ref[g * dk_a:g * dk_a + hd, :] = o1
        kat_ref[g * dk_a + hd:(g + 1) * dk_a, :] = o2
        kt_ref[g * dk_a:g * dk_a + hd, :] = o1.astype(BF16)
        kt_ref[g * dk_a + hd:(g + 1) * dk_a, :] = o2.astype(BF16)
    r1, r2 = _rope_rows(kk[n_a:n_a + 16], kk[n_a + 16:n_a + 32], cosbt_ref[...], sinbt_ref[...])
    krt_ref[0:16, :] = r1
    krt_ref[16:32, :] = r2

    v = p[:, n_a:2 * n_a]
    for h in range(h_a):
        va_ref[pl.ds(h, tm, stride=h_a), :] = v[:, _slab(h)]
    v_ref[:, 0:n_a] = v.astype(BF16)
    ckv = _rms(p[:, 2 * n_a + q_lora:]) * gkv_ref[...]
    ckv_ref[...] = ckv
    ckv_bf = ckv.astype(BF16)
    v_ref[:, n_a:] = _dot(ckv_bf, wuv_ref[...]).astype(BF16)

    knt = _dot_nt(wukt_ref[...], ckv_bf)
    r1b, r2b = r1.astype(BF16), r2.astype(BF16)
    zero = jnp.zeros((32, tm), BF16)
    for h in range(h_b):
        base = n_a + h * LANES
        kt_ref[base:base + 64, :] = knt[h * 64:(h + 1) * 64].astype(BF16)
        kt_ref[base + 64:base + 80, :] = r1b
        kt_ref[base + 80:base + 96, :] = r2b
        kt_ref[base + 96:base + 128, :] = zero


def _proj_prompt(cfg, x3, mod3, tables, weights, batch, seq):
    ng, tm, d = x3.shape
    n = ng * tm
    tpb = seq // tm
    h_a, dk_a, h_b, q_lora, kv_lora, _, _ = cfg
    n_a = h_a * 2 * dk_a
    tab = pl.BlockSpec((tm, LANES), lambda i: (i % tpb, 0))
    tabt = lambda a: pl.BlockSpec((a.shape[0], tm), lambda i: (0, i % tpb))
    in_specs = [pl.BlockSpec((1, tm, d), lambda i: (i, 0, 0)),
                pl.BlockSpec((1, 6, d), lambda i: (i // tpb, 0, 0)),
                tab, tab, tab, tab] + [tabt(a) for a in tables[4:]] + [_full(w) for w in weights]
    row = lambda w: pl.BlockSpec((tm, w), lambda i: (i, 0))
    nq = (2 * h_a + h_b) * LANES
    nk = n_a + h_b * LANES
    nv = n_a + h_b * 64
    out_specs = [row(nq),
                 pl.BlockSpec((None, n_a, tm), lambda i: (i // tpb, 0, i % tpb)),
                 pl.BlockSpec((None, 32, tm), lambda i: (i // tpb, 0, i % tpb)),
                 pl.BlockSpec((None, None, nk, tm), lambda i: (i // tpb, i % tpb, 0, 0)),
                 row(nv),
                 pl.BlockSpec((tm * h_a, LANES), lambda i: (i, 0)),
                 row(kv_lora)]
    out_shape = [jax.ShapeDtypeStruct((n, nq), BF16),
                 jax.ShapeDtypeStruct((batch, n_a, seq), F32),
                 jax.ShapeDtypeStruct((batch, 32, seq), F32),
                 jax.ShapeDtypeStruct((batch, tpb, nk, tm), BF16),
                 jax.ShapeDtypeStruct((n, nv), BF16),
                 jax.ShapeDtypeStruct((n * h_a, LANES), F32),
                 jax.ShapeDtypeStruct((n, kv_lora), F32)]
    return pl.pallas_call(
        functools.partial(_proj_prompt_kernel, cfg),
        grid=(ng,),
        in_specs=in_specs,
        out_specs=out_specs,
        out_shape=out_shape,
        compiler_params=_cparams(("parallel",)),
        name="proj_prompt",
    )(x3, mod3, *tables, *weights)


def _proj_sample_kernel(cfg, x_ref, mod_ref, cosa_ref, sina_ref, cosb_ref, sinb_ref, win_ref, gq_ref, wuq_ref,
                        gkv_ref, wabs_ref, sel_ref, qa_ref, ka_ref, va_ref, ckv_ref, kr_ref, qlat_ref, qrope_ref):
    h_a, dk_a, h_b, q_lora, kv_lora, scale_a, scale_b = cfg
    n_a = h_a * 2 * dk_a
    off_cq = 3 * n_a
    off_ckv = off_cq + q_lora
    off_kr = off_ckv + kv_lora
    hmod = _modulated(x_ref, mod_ref)
    p = _dot(hmod, win_ref[...])
    cosa, sina, cosb, sinb = cosa_ref[...], sina_ref[...], cosb_ref[...], sinb_ref[...]
    for s in range(n_a // LANES):
        qa_ref[:, _slab(s)] = _rope(p[:, _slab(s)], cosa, sina, dk_a // 2) * scale_a
        ka_ref[:, _slab(s)] = _rope(p[:, n_a + s * LANES:n_a + (s + 1) * LANES], cosa, sina, dk_a // 2)
    va_ref[...] = p[:, 2 * n_a:3 * n_a]
    cq = _rms(p[:, off_cq:off_ckv]) * gq_ref[...]
    qb = _dot(cq.astype(BF16), wuq_ref[...])
    ckv_ref[...] = _rms(p[:, off_ckv:off_kr]) * gkv_ref[...]
    kr_ref[...] = _rope(p[:, off_kr:off_kr + LANES], cosb, sinb, 16)[:, 64:96]
    qbr = [(_rope(qb[:, _slab(h)], cosb, sinb, 16) * scale_b).astype(BF16) for h in range(h_b)]
    for h in range(h_b):
        qlat_ref[:, h * kv_lora:(h + 1) * kv_lora] = _dot(qbr[h], wabs_ref[h])
    qrope_ref[...] = _dot(jnp.concatenate(qbr, axis=1), sel_ref[...])


def _proj_sample(cfg, x3, mod3, tables, weights, g_blk):
    ng, r, d = x3.shape
    tm = g_blk * r
    n = ng * r
    h_a, dk_a, h_b, q_lora, kv_lora, _, _ = cfg
    n_a = h_a * 2 * dk_a
    tab = pl.BlockSpec((tm, LANES), lambda i: (0, 0))
    in_specs = [pl.BlockSpec((g_blk, r, d), lambda i: (i, 0, 0)),
                pl.BlockSpec((g_blk, 6, d), lambda i: (i, 0, 0)),
                tab, tab, tab, tab] + [_full(w) for w in weights]
    widths = [n_a, n_a, n_a, kv_lora, 32, h_b * kv_lora, h_b * 32]
    return pl.pallas_call(
        functools.partial(_proj_sample_kernel, cfg),
        grid=(ng // g_blk,),
        in_specs=in_specs,
        out_specs=[pl.BlockSpec((tm, w), lambda i: (i, 0)) for w in widths],
        out_shape=[jax.ShapeDtypeStruct((n, w), F32) for w in widths],
        compiler_params=_cparams(("parallel",)),
        name="proj_sample",
    )(x3, mod3, *tables, *weights)


def _prompt_attn_kernel(h_a, h_b, lam_init, tq, q_ref, kt_ref, v_ref, lamv_ref, oa_ref, ob_ref,
                        m_ref, l_ref, acc_ref):
    qi = pl.program_id(1)
    n_streams = 2 * h_a + h_b
    row = lax.broadcasted_iota(jnp.int32, (tq, tq), 0)
    col = lax.broadcasted_iota(jnp.int32, (tq, tq), 1)
    causal = col <= row

    def slabs(st):
        if st < 2 * h_a:
            return st // 2, st // 2
        h = st - 2 * h_a
        return h_a + h, h_a + h // 2

    m_ref[...] = jnp.full(m_ref.shape, NEG_INF, F32)
    l_ref[...] = jnp.zeros(l_ref.shape, F32)
    acc_ref[...] = jnp.zeros(acc_ref.shape, F32)

    def block(js, masked):
        for st in range(n_streams):
            ks, vs = slabs(st)
            q = q_ref[:, _slab(st)]
            ss = [_dot(q, kt_ref[j, _slab(ks), :]) for j in js]
            if masked:
                ss[-1] = jnp.where(causal, ss[-1], NEG_INF)
            s = jnp.concatenate(ss, axis=1)
            n_tiles = s.shape[1] // LANES
            m_old = m_ref[st]
            m_new = jnp.maximum(m_old, jnp.max(s, axis=-1, keepdims=True))
            alpha = jnp.exp2(m_old - m_new)
            p = jnp.exp2(s - jnp.concatenate([m_new] * n_tiles, axis=1))
            psum = p[:, 0:LANES]
            for c in range(1, n_tiles):
                psum = psum + p[:, _slab(c)]
            l_ref[st] = alpha * l_ref[st] + psum
            m_ref[st] = m_new
            pv = None
            for i, j in enumerate(js):
                k0 = pl.multiple_of(j * tq, tq)
                d = _dot(p[:, i * tq:(i + 1) * tq].astype(BF16), v_ref[pl.ds(k0, tq), _slab(vs)])
                pv = d if pv is None else pv + d
            acc_ref[st] = alpha * acc_ref[st] + pv

    def body(jj, carry):
        block([2 * jj], False)
        block([2 * jj + 1], False)
        return carry
    lax.fori_loop(0, qi // 2, body, 0)

    @pl.when(qi % 2 == 1)
    def _():
        block([qi - 1], False)
    block([qi], True)

    lam = _lambda(lamv_ref[...], lam_init)
    out = lambda st: acc_ref[st] / jnp.sum(l_ref[st], axis=-1, keepdims=True)
    for h in range(h_a):
        oa_ref[:, _slab(h)] = out(2 * h) - lam * out(2 * h + 1)
    low = lax.broadcasted_iota(jnp.int32, (tq, LANES), 1) < 64
    for hp in range(h_b // 2):
        ob_ref[:, _slab(hp)] = jnp.where(low, out(2 * h_a + 2 * hp), out(2 * h_a + 2 * hp + 1)).astype(BF16)


def _prompt_attn(h_a, h_b, lam_init, batch, seq, tq, q, kt, v, lamv):
    nq = seq // tq
    n = batch * seq
    n_streams = 2 * h_a + h_b
    qspec = lambda w: pl.BlockSpec((tq, w), lambda b, i: (b * nq + i, 0))
    return pl.pallas_call(
        functools.partial(_prompt_attn_kernel, h_a, h_b, lam_init, tq),
        grid=(batch, nq),
        in_specs=[qspec(q.shape[1]),
                  pl.BlockSpec((None,) + kt.shape[1:], lambda b, i: (b, 0, 0, 0)),
                  pl.BlockSpec((seq, v.shape[1]), lambda b, i: (b, 0)),
                  _full(lamv)],
        out_specs=[qspec(h_a * LANES), qspec(h_b * 64)],
        out_shape=[jax.ShapeDtypeStruct((n, h_a * LANES), F32), jax.ShapeDtypeStruct((n, h_b * 64), BF16)],
        scratch_shapes=[pltpu.VMEM((n_streams, tq, LANES), F32)] * 3,
        compiler_params=_cparams(("parallel", "arbitrary")),
        name="prompt_attn",
    )(q, kt, v, lamv)


def _sample_attn_kernel(h_a, h_b, kv_lora, lam_init, final, npg, base, pt_ref, qa_ref, qlat_ref, qrope_ref, kn_ref,
                        vn_ref, cn_ref, rn_ref, lamv_ref, wuv_ref,
                        x_ref, mod_ref, poa_ref, pob_ref, gsub_ref, wout_ref, w1_ref, w2_ref, gfin_ref,
                        ck_hbm, cv_hbm, cc_hbm, cr_hbm, oa_ref, ob_ref, y_ref,
                        qbd, ql, qr, m_a, l_a, acc_a, m_b, l_b, acc_b, kbuf, vbuf, cbuf, rbuf, sem,
                        x1_scr, hm_scr, ff_scr):
    i, j = pl.program_id(0), pl.program_id(1)
    steps = pl.num_programs(1)
    total = pl.num_programs(0) * steps
    n = i * steps + j
    slot = n % 2
    t = qa_ref.shape[0]
    page = cbuf.shape[2]
    rows_a = 2 * h_a * t
    rows_b = h_b * t
    caches, bufs = (ck_hbm, cv_hbm, cc_hbm, cr_hbm), (kbuf, vbuf, cbuf, rbuf)

    def page_copies(step, dst_slot):
        si, sj = step // steps, step % steps
        copies = []
        for k in range(npg):
            pid = base + pt_ref[si, sj * npg + k]
            for kind in range(4):
                copies.append(pltpu.make_async_copy(caches[kind].at[pid], bufs[kind].at[dst_slot, k],
                                                    sem.at[dst_slot, kind]))
        return copies

    @pl.when(n == 0)
    def _():
        for c in page_copies(0, 0):
            c.start()

        @pl.when(total > 1)
        def _():
            for c in page_copies(1, 1):
                c.start()

    def softmax_step(m_ref, l_ref, s):
        m_old = m_ref[...]
        m_new = jnp.maximum(m_old, jnp.max(s, axis=-1, keepdims=True))
        alpha = jnp.exp(m_old - m_new)
        p = jnp.exp(s - m_new)
        l_ref[...] = alpha * l_ref[...] + jnp.sum(p, axis=-1, keepdims=True)
        m_ref[...] = m_new
        return alpha, p

    def attend(sa, sb, n_chunks, val_a, val_b):
        alpha, p = softmax_step(m_a, l_a, sa)
        pv = []
        for h in range(h_a):
            ph = p[2 * t * h:2 * t * (h + 1)]
            acc = _dot(ph[:, 0:page].astype(BF16), val_a(h, 0))
            for i in range(1, n_chunks):
                acc = acc + _dot(ph[:, i * page:(i + 1) * page].astype(BF16), val_a(h, i))
            pv.append(acc)
        acc_a[...] = alpha * acc_a[...] + jnp.concatenate(pv, axis=0)
        alpha, p = softmax_step(m_b, l_b, sb)
        acc = _dot(p[:, 0:page].astype(BF16), val_b(0))
        for i in range(1, n_chunks):
            acc = acc + _dot(p[:, i * page:(i + 1) * page].astype(BF16), val_b(i))
        acc_b[...] = alpha * acc_b[...] + acc

    @pl.when(j == 0)
    def _():
        qa = qa_ref[...]
        r = lax.broadcasted_iota(jnp.int32, (rows_a, qa.shape[1]), 0)
        c = lax.broadcasted_iota(jnp.int32, (rows_a, qa.shape[1]), 1)
        dk = qa.shape[1] // (2 * h_a)
        qbd[...] = jnp.where((r >> _log2(t)) == (c >> _log2(dk)),
                             jnp.concatenate([qa] * (2 * h_a), axis=0), 0.0).astype(BF16)
        ql[...] = jnp.concatenate([qlat_ref[:, h * kv_lora:(h + 1) * kv_lora] for h in range(h_b)],
                                  axis=0).astype(BF16)
        qr[...] = jnp.concatenate([qrope_ref[:, h * 32:(h + 1) * 32] for h in range(h_b)], axis=0).astype(BF16)
        for m_ref, l_ref, acc_ref in ((m_a, l_a, acc_a), (m_b, l_b, acc_b)):
            m_ref[...] = jnp.full(m_ref.shape, NEG_INF, F32)
            l_ref[...] = jnp.zeros(l_ref.shape, F32)
            acc_ref[...] = jnp.zeros(acc_ref.shape, F32)
        pad = lambda ref: jnp.concatenate(
            [ref[...], jnp.zeros((page - t, ref.shape[1]), F32)], axis=0).astype(BF16)
        kn, vn, cn, rn = pad(kn_ref), pad(vn_ref), pad(cn_ref), pad(rn_ref)
        ca = lax.broadcasted_iota(jnp.int32, (rows_a, page), 1)
        ra = lax.broadcasted_iota(jnp.int32, (rows_a, page), 0)
        cb = lax.broadcasted_iota(jnp.int32, (rows_b, page), 1)
        rb = lax.broadcasted_iota(jnp.int32, (rows_b, page), 0)
        sa = jnp.where(ca <= (ra & (t - 1)), _dot_nt(qbd[...], kn), NEG_INF)
        sb = jnp.where(cb <= (rb & (t - 1)), _dot_nt(ql[...], cn) + _dot_nt(qr[...], rn), NEG_INF)
        attend(sa, sb, 1, lambda h, i: vn[:, _slab(h)], lambda i: cn)

    for c in page_copies(n, slot):
        c.wait()
    cs = [cbuf[slot, k].astype(BF16) for k in range(npg)]
    sa = jnp.concatenate([_dot(qbd[...], kbuf[slot, k].astype(BF16)) for k in range(npg)], axis=1)
    sb = jnp.concatenate([_dot_nt(ql[...], cs[k]) + _dot(qr[...], rbuf[slot, k].astype(BF16))
                          for k in range(npg)], axis=1)
    attend(sa, sb, npg, lambda h, k: vbuf[slot, k, pl.ds(h, page, stride=h_a), :].astype(BF16), lambda k: cs[k])

    @pl.when(j == pl.num_programs(1) - 1)
    def _():
        lam = _lambda(lamv_ref[...], lam_init)
        oa = acc_a[...] / l_a[...]
        for h in range(h_a):
            oa_ref[:, _slab(h)] = oa[(2 * h) * t:(2 * h + 1) * t] - lam * oa[(2 * h + 1) * t:(2 * h + 2) * t]
        olat = (acc_b[...] / l_b[...]).astype(BF16)
        ofull = _dot(olat, wuv_ref[...])
        cc = lax.broadcasted_iota(jnp.int32, (t, ofull.shape[1]), 1) >> 6
        ob = jnp.zeros((t, ofull.shape[1]), F32)
        for h in range(h_b):
            ob = jnp.where(cc == h, ofull[h * t:(h + 1) * t, :], ob)
        ob_ref[...] = ob

    @pl.when(n + 2 < total)
    def _():
        for c in page_copies(n + 2, slot):
            c.start()

    mod = mod_ref[...]
    out_scale = 1.0 - lam_init

    @pl.when(j == 0)
    def _():
        x1, hm = _mix_in(h_a, out_scale, x_ref[...], mod, poa_ref, pob_ref, gsub_ref, wout_ref)
        x1_scr[...] = x1
        hm_scr[...] = hm
        ff_scr[...] = jnp.zeros(ff_scr.shape, F32)
    ff_scr[...] += _ffn_chunk(hm_scr[...], w1_ref[j], w2_ref[j])

    @pl.when(j == steps - 1)
    def _():
        y_ref[...] = _mix_out(final, x1_scr[...], mod, ff_scr[...], gfin_ref)


def _sample_attn(h_a, h_b, kv_lora, lam_init, final, base, page_table, qa3, qlat3, qrope3, kn3, vn3, cn3, rn3,
                 lamv, wuv, cache_kt, cache_v, cache_c, cache_rt, xp3, mod_p, poa, pob, gsub, wout, w1, w2, gfin):
    s, t, n_a = qa3.shape
    n_pages = page_table.shape[1]
    steps = w1.shape[0]
    npg = n_pages // steps
    n_tiles, tm, d = xp3.shape
    assert n_tiles == s and n_pages % steps == 0
    tiles_per_mod = n_tiles // mod_p.shape[0]
    seq = lambda a: pl.BlockSpec((None,) + a.shape[1:], lambda i, j, pt: (i, 0, 0))
    once = lambda a: pl.BlockSpec(a.shape, lambda i, j, pt, nd=a.ndim: (0,) * nd, pipeline_mode=pl.Buffered(1))
    caches = (cache_kt, cache_v, cache_c, cache_rt)
    in_specs = [seq(qa3), seq(qlat3), seq(qrope3), seq(kn3), seq(vn3), seq(cn3), seq(rn3), _full(lamv), _full(wuv)]
    in_specs += [pl.BlockSpec((1, tm, d), lambda i, j, pt: (i, 0, 0)),
                 pl.BlockSpec((1, 6, d), lambda i, j, pt: (i // tiles_per_mod, 0, 0)),
                 pl.BlockSpec((tm, poa.shape[1]), lambda i, j, pt: (i, 0)),
                 pl.BlockSpec((tm, pob.shape[1]), lambda i, j, pt: (i, 0)),
                 _full(gsub), once(wout), once(w1), once(w2), _full(gfin)]
    in_specs += [pl.BlockSpec(memory_space=pl.ANY)] * len(caches)
    rows_a, rows_b = 2 * h_a * t, h_b * t
    dv_a = cache_v.shape[2]
    scratch = [pltpu.VMEM((rows_a, n_a), BF16), pltpu.VMEM((rows_b, kv_lora), BF16), pltpu.VMEM((rows_b, 32), BF16),
               pltpu.VMEM((rows_a, 1), F32), pltpu.VMEM((rows_a, 1), F32), pltpu.VMEM((rows_a, dv_a), F32),
               pltpu.VMEM((rows_b, 1), F32), pltpu.VMEM((rows_b, 1), F32), pltpu.VMEM((rows_b, kv_lora), F32)]
    scratch += [pltpu.VMEM((2, npg) + a.shape[1:], F32) for a in caches]
    scratch += [pltpu.SemaphoreType.DMA((2, len(caches)))]
    scratch += [pltpu.VMEM((1, tm, d), F32), pltpu.VMEM((tm, d), BF16), pltpu.VMEM((tm, d), F32)]
    out_wa, out_wb = h_a * dv_a, wuv.shape[1]
    grid_spec = pltpu.PrefetchScalarGridSpec(
        num_scalar_prefetch=1,
        grid=(s, steps),
        in_specs=in_specs,
        out_specs=[pl.BlockSpec((None, t, out_wa), lambda i, j, pt: (i, 0, 0)),
                   pl.BlockSpec((None, t, out_wb), lambda i, j, pt: (i, 0, 0)),
                   pl.BlockSpec((1, tm, d), lambda i, j, pt: (i, 0, 0))],
        scratch_shapes=scratch)
    return pl.pallas_call(
        functools.partial(_sample_attn_kernel, h_a, h_b, kv_lora, lam_init, final, npg, base),
        grid_spec=grid_spec,
        out_shape=[jax.ShapeDtypeStruct((s, t, out_wa), F32), jax.ShapeDtypeStruct((s, t, out_wb), F32),
                   jax.ShapeDtypeStruct(xp3.shape, F32)],
        compiler_params=_cparams(("arbitrary", "arbitrary")),
        name="sample_attn",
    )(page_table, qa3, qlat3, qrope3, kn3, vn3, cn3, rn3, lamv, wuv, xp3, mod_p, poa, pob, gsub, wout, w1, w2,
      gfin, *caches)


def _mix_ffn_kernel(h_a, out_scale, final, x_ref, mod_ref, oa_ref, ob_ref, gsub_ref, wout_ref,
                    w1_ref, w2_ref, gfin_ref, y_ref):
    mod = mod_ref[...]
    x1, hm = _mix_in(h_a, out_scale, x_ref[...], mod, oa_ref, ob_ref, gsub_ref, wout_ref)
    ff = _ffn_chunk(hm, w1_ref[0], w2_ref[0])
    for c in range(1, w1_ref.shape[0]):
        ff = ff + _ffn_chunk(hm, w1_ref[c], w2_ref[c])
    y_ref[...] = _mix_out(final, x1, mod, ff, gfin_ref)


def _mix_in(h_a, out_scale, x, mod, oa_ref, ob_ref, gsub_ref, wout_ref):
    g, r, d = x.shape
    gsub = gsub_ref[...] * out_scale
    parts = [(_rms(oa_ref[:, _slab(h)]) * gsub).astype(BF16) for h in range(h_a)]
    parts.append(ob_ref[...].astype(BF16))
    att = _dot(jnp.concatenate(parts, axis=1), wout_ref[...])
    x1 = x + mod[:, 2:3, :] * att.reshape(g, r, d)
    hm = (_rms(x1) * (1.0 + mod[:, 4:5, :]) + mod[:, 3:4, :]).reshape(g * r, d).astype(BF16)
    return x1, hm


def _ffn_chunk(hm, w1c, w2c):
    a = _dot(hm, w1c)
    return _dot(jnp.square(jnp.maximum(a, 0.0)).astype(BF16), w2c)


def _mix_out(final, x1, mod, ff, gfin_ref):
    x2 = x1 + mod[:, 5:6, :] * ff.reshape(x1.shape)
    if final:
        x2 = _rms(x2) * gfin_ref[...]
    return x2


def _mix_ffn(h_a, out_scale, final, x3, mod3, oa, ob, gsub, wout, w1, w2, gfin, g_blk, tiles_per_mod):
    ng, r, d = x3.shape
    tm = g_blk * r
    row = lambda a: pl.BlockSpec((tm, a.shape[1]), lambda i: (i, 0))
    return pl.pallas_call(
        functools.partial(_mix_ffn_kernel, h_a, out_scale, final),
        grid=(ng // g_blk,),
        in_specs=[pl.BlockSpec((g_blk, r, d), lambda i: (i, 0, 0)),
                  pl.BlockSpec((g_blk, 6, d), lambda i: (i // tiles_per_mod, 0, 0)),
                  row(oa), row(ob), _full(gsub), _full(wout), _full(w1), _full(w2), _full(gfin)],
        out_specs=pl.BlockSpec((g_blk, r, d), lambda i: (i, 0, 0)),
        out_shape=jax.ShapeDtypeStruct((ng, r, d), F32),
        compiler_params=_cparams(("parallel",)),
        name="mix_ffn",
    )(x3, mod3, oa, ob, gsub, wout, w1, w2, gfin)


def _rope_angles(pos, d):
    inv = 1.0 / (ROPE_THETA ** (jnp.arange(0, d, 2, dtype=F32) / d))
    ang = pos.astype(F32)[:, None] * inv[None, :]
    return jnp.cos(ang), jnp.sin(ang)


def _rope_tables(pos, d_a, d_b):
    (ca, sa), (cb, sb) = _rope_angles(pos, d_a), _rope_angles(pos, d_b)
    n = pos.shape[0]
    reps = LANES // d_a
    cosa = jnp.tile(jnp.concatenate([ca, ca], axis=1), (1, reps))
    sina = jnp.tile(jnp.concatenate([-sa, sa], axis=1), (1, reps))
    one, zero = jnp.ones((n, 64), F32), jnp.zeros((n, 64), F32)
    cosb = jnp.concatenate([one, cb, cb, one[:, :LANES - 64 - d_b]], axis=1)
    sinb = jnp.concatenate([zero, -sb, sb, zero[:, :LANES - 64 - d_b]], axis=1)
    return cosa, sina, cosb, sinb


def kernel(x_prompt, x_sample, c_prompt, c_sample, cache_diff_k, cache_diff_v, cache_mla_ckv, cache_mla_krope,
           page_table, w_ada, b_ada, w_in, g_q, w_uq, w_uk, g_kv, lambda_q1, lambda_k1, lambda_q2, lambda_k2,
           g_sub, w_uv, w_out, w_ff1, w_ff2, g_final):
    batch, seq, d = x_prompt.shape
    s_batch, s_seq, _ = x_sample.shape
    depth, n_phys, page, h_a, _, dk_a = cache_diff_k.shape
    dv_a = cache_diff_v.shape[-1]
    kv_lora = cache_mla_ckv.shape[-1]
    qk_rope = cache_mla_krope.shape[-1]
    h_b, _, qk_nope = w_uk.shape[1:]
    v_head = w_uv.shape[-1]
    q_lora = g_q.shape[-1]
    past_len = page_table.shape[1] * page
    n_a = h_a * 2 * dk_a
    assert (dk_a, dv_a, qk_nope, qk_rope, v_head) == (64, 128, 64, 32, 64), "head geometry the lane layouts assume"
    assert w_in.shape[-1] == 3 * n_a + q_lora + kv_lora + qk_rope
    cfg = (h_a, dk_a, h_b, q_lora, kv_lora, dk_a ** -0.5, (qk_nope + qk_rope) ** -0.5)

    tm_p, ts, ff_chunk = batch * seq // s_batch, 32, 1024
    assert tm_p * s_batch == batch * seq and seq % tm_p == 0 and tm_p % LANES == 0
    pos_p = jnp.arange(seq)
    (ca_p, sa_p), (cb_p, sb_p) = _rope_angles(pos_p, dk_a), _rope_angles(pos_p, qk_rope)
    tab_p = _rope_tables(pos_p, dk_a, qk_rope) + (ca_p.T, sa_p.T, cb_p.T, sb_p.T)
    tab_s = tuple(jnp.tile(a, (ts, 1)) for a in _rope_tables(past_len + jnp.arange(s_seq), dk_a, qk_rope))

    xp3 = x_prompt.reshape(batch * seq // tm_p, tm_p, d)
    xs3 = x_sample
    c_all = jnp.concatenate([c_prompt, c_sample], axis=0)
    ckt = jnp.transpose(cache_diff_k, (0, 1, 3, 4, 5, 2)).reshape(depth * n_phys, n_a, page)
    cv = cache_diff_v.reshape(depth * n_phys, page * h_a, dv_a)
    cc = cache_mla_ckv.reshape(depth * n_phys, page, kv_lora)
    crt = jnp.transpose(cache_mla_krope, (0, 1, 3, 2)).reshape(depth * n_phys, qk_rope, page)

    sel_rows = (jnp.arange(h_b)[:, None] * LANES + 64 + jnp.arange(qk_rope)[None, :]).reshape(-1)
    sel = jnp.zeros((h_b * LANES, h_b * qk_rope), BF16).at[sel_rows, jnp.arange(h_b * qk_rope)].set(1.0)

    new_p, new_s = [], []
    for l in range(depth):
        lam_init = 0.8 - 0.6 * math.exp(-0.3 * l)
        mod = _adaln(c_all, w_ada[l], b_ada[l]).reshape(batch + s_batch, 6, d)
        mod_p, mod_s = mod[:batch], mod[batch:]
        off_ckv = 3 * n_a + q_lora
        off_kr = off_ckv + kv_lora
        w_l = w_in[l]
        win_s = jnp.concatenate([w_l[:, :off_kr], jnp.zeros((d, 64), F32), w_l[:, off_kr:],
                                 jnp.zeros((d, LANES - 64 - qk_rope), F32)], axis=1).astype(BF16)
        wmain = jnp.concatenate([w_l[:, :n_a], w_l[:, 2 * n_a:off_kr]], axis=1).astype(BF16)
        wkt = jnp.concatenate([w_l[:, n_a:2 * n_a], w_l[:, off_kr:]], axis=1).T.astype(BF16)
        wuq = jnp.pad(w_uq[l].reshape(q_lora, h_b, qk_nope + qk_rope),
                      ((0, 0), (0, 0), (0, LANES - qk_nope - qk_rope))).reshape(q_lora, h_b * LANES).astype(BF16)
        wukt = jnp.transpose(w_uk[l], (0, 2, 1)).reshape(h_b * qk_nope, kv_lora).astype(BF16)
        wuv_all = jnp.transpose(w_uv[l], (1, 0, 2)).reshape(kv_lora, h_b * v_head).astype(BF16)
        wabs = jnp.pad(jnp.transpose(w_uk[l], (0, 2, 1)), ((0, 0), (0, LANES - qk_nope), (0, 0))).astype(BF16)
        gq, gkv, gsub = g_q[l].reshape(1, -1), g_kv[l].reshape(1, -1), g_sub[l].reshape(1, -1)
        lamv = jnp.stack([lambda_q1[l], lambda_k1[l], lambda_q2[l], lambda_k2[l]], axis=0)
        d_ff = w_ff1.shape[-1]
        n_ch = d_ff // ff_chunk
        wout = w_out[l].astype(BF16)
        w1 = jnp.transpose(w_ff1[l].reshape(d, n_ch, ff_chunk), (1, 0, 2)).astype(BF16)
        w2 = w_ff2[l].reshape(n_ch, ff_chunk, d).astype(BF16)
        gfin = g_final.reshape(1, d)
        final = l == depth - 1

        q, kat, krt, kt, v, va, ckv = _proj_prompt(
            cfg, xp3, mod_p, tab_p, (wmain, wkt, gq, wuq, gkv, wukt, wuv_all), batch, seq)
        oa, ob = _prompt_attn(h_a, h_b, lam_init, batch, seq, tm_p, q, kt, v, lamv)
        new_p.append((jnp.transpose(kat.reshape(batch, h_a, 2, dk_a, seq), (0, 4, 1, 2, 3)),
                      va.reshape(batch, seq, h_a, dv_a),
                      ckv.reshape(batch, seq, kv_lora),
                      jnp.transpose(krt, (0, 2, 1))))

        sqa, ska, sva, sckv, skr, sqlat, sqrope = _proj_sample(
            cfg, xs3, mod_s, tab_s, (win_s, gq, wuq, gkv, wabs, sel), ts)
        r3 = lambda a: a.reshape(s_batch, s_seq, a.shape[-1])
        soa, sob, xp3 = _sample_attn(h_a, h_b, kv_lora, lam_init, final, l * n_phys, page_table, r3(sqa), r3(sqlat),
                                     r3(sqrope), r3(ska), r3(sva), r3(sckv), r3(skr), lamv, wuv_all, ckt, cv, cc, crt,
                                     xp3, mod_p, oa, ob, gsub, wout, w1, w2, gfin)
        n_s = s_batch * s_seq
        xs3 = _mix_ffn(h_a, 1.0 - lam_init, final, xs3, mod_s, soa.reshape(n_s, -1), sob.reshape(n_s, -1),
                       gsub, wout, w1, w2, gfin, ts, 1)
        new_s.append((ska.reshape(s_batch, s_seq, h_a, 2, dk_a), sva.reshape(s_batch, s_seq, h_a, dv_a),
                      sckv.reshape(s_batch, s_seq, kv_lora), skr.reshape(s_batch, s_seq, qk_rope)))

    y_prompt = xp3.reshape(batch, seq, d)
    y_sample = xs3
    stack = lambda rows, i: jnp.stack([r[i] for r in rows], axis=0)
    return (y_prompt, y_sample, stack(new_p, 0), stack(new_p, 1), stack(new_p, 2), stack(new_p, 3),
            stack(new_s, 0), stack(new_s, 1), stack(new_s, 2), stack(new_s, 3))
```

```python
import functools
import math

import jax
import jax.numpy as jnp
import numpy as np
from jax import lax
from jax.experimental import pallas as pl
from jax.experimental.pallas import tpu as pltpu

F32 = jnp.float32
BF16 = jnp.bfloat16

ROPE_THETA = 10000.0
EPS = 1e-6
NEG_INF = -1e30
LOG2E = math.log2(math.e)
LANES = 128
VMEM_LIMIT = 56 * 1024 * 1024


def _cparams(sem):
    return pltpu.CompilerParams(dimension_semantics=sem, vmem_limit_bytes=VMEM_LIMIT)


def _rms(x):
    return x * lax.rsqrt(jnp.mean(x * x, axis=-1, keepdims=True) + EPS)


def _dot(a, b):
    return jnp.dot(a, b, preferred_element_type=F32)


def _dot_nt(a, b):
    return lax.dot_general(a, b, (((1,), (1,)), ((), ())), preferred_element_type=F32)


def _slab(i, w=LANES):
    return slice(i * w, (i + 1) * w)


def _log2(n):
    assert n & (n - 1) == 0, "power of two expected"
    return n.bit_length() - 1


def _rope(x, cos, sin_signed, half):
    lane = lax.broadcasted_iota(jnp.int32, x.shape, 1)
    first = (lane & (2 * half - 1)) < half
    swapped = jnp.where(first, pltpu.roll(x, LANES - half, 1), pltpu.roll(x, half, 1))
    return x * cos + swapped * sin_signed


def _rope_rows(x1, x2, cos, sin):
    return x1 * cos - x2 * sin, x2 * cos + x1 * sin


def _lambda(lamv, lam_init):
    e1 = jnp.exp(jnp.sum(lamv[0:1] * lamv[1:2], axis=-1, keepdims=True))
    e2 = jnp.exp(jnp.sum(lamv[2:3] * lamv[3:4], axis=-1, keepdims=True))
    return e1 - e2 + lam_init


def _full(a):
    return pl.BlockSpec(a.shape, lambda *idx, nd=a.ndim: (0,) * nd)


def _adaln_kernel(c_ref, w_ref, b_ref, o_ref):
    c = c_ref[...]
    s = c / (1.0 + jnp.exp(-c))
    o_ref[...] = _dot(s.astype(BF16), w_ref[...].astype(BF16)) + b_ref[...]


def _adaln(c_all, w_ada, b_ada):
    n, d = c_all.shape
    d_out = w_ada.shape[1]
    tn = 1024
    return pl.pallas_call(
        _adaln_kernel,
        grid=(d_out // tn,),
        in_specs=[pl.BlockSpec((n, d), lambda j: (0, 0)),
                  pl.BlockSpec((d, tn), lambda j: (0, j)),
                  pl.BlockSpec((1, tn), lambda j: (0, j))],
        out_specs=pl.BlockSpec((n, tn), lambda j: (0, j)),
        out_shape=jax.ShapeDtypeStruct((n, d_out), F32),
        compiler_params=_cparams(("arbitrary",)),
        name="adaln",
    )(c_all, w_ada, b_ada.reshape(1, d_out))


def _modulated(x_ref, mod_ref):
    x = x_ref[...]
    g, r, d = x.shape
    mod = mod_ref[...]
    h = _rms(x) * (1.0 + mod[:, 1:2, :]) + mod[:, 0:1, :]
    return h.reshape(g * r, d).astype(BF16)


def _proj_prompt_kernel(cfg, x_ref, mod_ref, cosa_ref, sina_ref, cosb_ref, sinb_ref, cosat_ref, sinat_ref,
                        cosbt_ref, sinbt_ref, wmain_ref, wkt_ref, gq_ref, wuq_ref, gkv_ref, wukt_ref, wuv_ref,
                        q_ref, kat_ref, krt_ref, kt_ref, v_ref, va_ref, ckv_ref):
    h_a, dk_a, h_b, q_lora, kv_lora, scale_a, scale_b = cfg
    scale_a, scale_b = scale_a * LOG2E, scale_b * LOG2E
    n_a = h_a * 2 * dk_a
    hmod = _modulated(x_ref, mod_ref)
    tm = hmod.shape[0]
    p = _dot(hmod, wmain_ref[...])
    kk = _dot_nt(wkt_ref[...], hmod)
    cosa, sina, cosb, sinb = cosa_ref[...], sina_ref[...], cosb_ref[...], sinb_ref[...]
    lane = lax.broadcasted_iota(jnp.int32, (tm, LANES), 1)
    low = lane < dk_a

    for h in range(h_a):
        q = _rope(p[:, _slab(h)], cosa, sina, dk_a // 2) * scale_a
        q_ref[:, _slab(2 * h)] = jnp.where(low, q, 0.0).astype(BF16)
        q_ref[:, _slab(2 * h + 1)] = jnp.where(low, 0.0, q).astype(BF16)
    cq = _rms(p[:, 2 * n_a:2 * n_a + q_lora]) * gq_ref[...]
    qb = _dot(cq.astype(BF16), wuq_ref[...])
    for h in range(h_b):
        q_ref[:, _slab(2 * h_a + h)] = (_rope(qb[:, _slab(h)], cosb, sinb, 16) * scale_b).astype(BF16)

    ca, sa = cosat_ref[...], sinat_ref[...]
    hd = dk_a // 2
    for g in range(n_a // dk_a):
        o1, o2 = _rope_rows(kk[g * dk_a:g * dk_a + hd], kk[g * dk_a + hd:(g + 1) * dk_a], ca, sa)
        kat_ref[g * dk_a:g * dk_a + hd, :] = o1
        kat_ref[g * dk_a + hd:(g + 1) * dk_a, :] = o2
        kt_ref[g * dk_a:g * dk_a + hd, :] = o1.astype(BF16)
        kt_ref[g * dk_a + hd:(g + 1) * dk_a, :] = o2.astype(BF16)
    r1, r2 = _rope_rows(kk[n_a:n_a + 16], kk[n_a + 16:n_a + 32], cosbt_ref[...], sinbt_ref[...])
    krt_ref[0:16, :] = r1
    krt_ref[16:32, :] = r2

    v = p[:, n_a:2 * n_a]
    for h in range(h_a):
        va_ref[pl.ds(h, tm, stride=h_a), :] = v[:, _slab(h)]
    v_ref[:, 0:n_a] = v.astype(BF16)
    ckv = _rms(p[:, 2 * n_a + q_lora:]) * gkv_ref[...]
    ckv_ref[...] = ckv
    ckv_bf = ckv.astype(BF16)
    v_ref[:, n_a:] = _dot(ckv_bf, wuv_ref[...]).astype(BF16)

    knt = _dot_nt(wukt_ref[...], ckv_bf)
    r1b, r2b = r1.astype(BF16), r2.astype(BF16)
    zero = jnp.zeros((32, tm), BF16)
    for h in range(h_b):
        base = n_a + h * LANES
        kt_ref[base:base + 64, :] = knt[h * 64:(h + 1) * 64].astype(BF16)
        kt_ref[base + 64:base + 80, :] = r1b
        kt_ref[base + 80:base + 96, :] = r2b
        kt_ref[base + 96:base + 128, :] = zero


def _proj_prompt(cfg, x3, mod3, tables, weights, batch, seq):
    ng, tm, d = x3.shape
    n = ng * tm
    tpb = seq // tm
    h_a, dk_a, h_b, q_lora, kv_lora, _, _ = cfg
    n_a = h_a * 2 * dk_a
    tab = pl.BlockSpec((tm, LANES), lambda i: (i % tpb, 0))
    tabt = lambda a: pl.BlockSpec((a.shape[0], tm), lambda i: (0, i % tpb))
    in_specs = [pl.BlockSpec((1, tm, d), lambda i: (i, 0, 0)),
                pl.BlockSpec((1, 6, d), lambda i: (i // tpb, 0, 0)),
                tab, tab, tab, tab] + [tabt(a) for a in tables[4:]] + [_full(w) for w in weights]
    row = lambda w: pl.BlockSpec((tm, w), lambda i: (i, 0))
    nq = (2 * h_a + h_b) * LANES
    nk = n_a + h_b * LANES
    nv = n_a + h_b * 64
    out_specs = [row(nq),
                 pl.BlockSpec((None, n_a, tm), lambda i: (i // tpb, 0, i % tpb)),
                 pl.BlockSpec((None, 32, tm), lambda i: (i // tpb, 0, i % tpb)),
                 pl.BlockSpec((None, None, nk, tm), lambda i: (i // tpb, i % tpb, 0, 0)),
                 row(nv),
                 pl.BlockSpec((tm * h_a, LANES), lambda i: (i, 0)),
                 row(kv_lora)]
    out_shape = [jax.ShapeDtypeStruct((n, nq), BF16),
                 jax.ShapeDtypeStruct((batch, n_a, seq), F32),
                 jax.ShapeDtypeStruct((batch, 32, seq), F32),
                 jax.ShapeDtypeStruct((batch, tpb, nk, tm), BF16),
                 jax.ShapeDtypeStruct((n, nv), BF16),
                 jax.ShapeDtypeStruct((n * h_a, LANES), F32),
                 jax.ShapeDtypeStruct((n, kv_lora), F32)]
    return pl.pallas_call(
        functools.partial(_proj_prompt_kernel, cfg),
        grid=(ng,),
        in_specs=in_specs,
        out_specs=out_specs,
        out_shape=out_shape,
        compiler_params=_cparams(("parallel",)),
        name="proj_prompt",
    )(x3, mod3, *tables, *weights)


def _proj_sample_kernel(cfg, x_ref, mod_ref, cosa_ref, sina_ref, cosb_ref, sinb_ref, win_ref, gq_ref, wuq_ref,
                        gkv_ref, wabs_ref, sel_ref, qa_ref, ka_ref, va_ref, ckv_ref, kr_ref, qlat_ref, qrope_ref):
    h_a, dk_a, h_b, q_lora, kv_lora, scale_a, scale_b = cfg
    n_a = h_a * 2 * dk_a
    off_cq = 3 * n_a
    off_ckv = off_cq + q_lora
    off_kr = off_ckv + kv_lora
    hmod = _modulated(x_ref, mod_ref)
    p = _dot(hmod, win_ref[...])
    cosa, sina, cosb, sinb = cosa_ref[...], sina_ref[...], cosb_ref[...], sinb_ref[...]
    for s in range(n_a // LANES):
        qa_ref[:, _slab(s)] = _rope(p[:, _slab(s)], cosa, sina, dk_a // 2) * scale_a
        ka_ref[:, _slab(s)] = _rope(p[:, n_a + s * LANES:n_a + (s + 1) * LANES], cosa, sina, dk_a // 2)
    va_ref[...] = p[:, 2 * n_a:3 * n_a]
    cq = _rms(p[:, off_cq:off_ckv]) * gq_ref[...]
    qb = _dot(cq.astype(BF16), wuq_ref[...])
    ckv_ref[...] = _rms(p[:, off_ckv:off_kr]) * gkv_ref[...]
    kr_ref[...] = _rope(p[:, off_kr:off_kr + LANES], cosb, sinb, 16)[:, 64:96]
    qbr = [(_rope(qb[:, _slab(h)], cosb, sinb, 16) * scale_b).astype(BF16) for h in range(h_b)]
    for h in range(h_b):
        qlat_ref[:, h * kv_lora:(h + 1) * kv_lora] = _dot(qbr[h], wabs_ref[h])
    qrope_ref[...] = _dot(jnp.concatenate(qbr, axis=1), sel_ref[...])


def _proj_sample(cfg, x3, mod3, tables, weights, g_blk):
    ng, r, d = x3.shape
    tm = g_blk * r
    n = ng * r
    h_a, dk_a, h_b, q_lora, kv_lora, _, _ = cfg
    n_a = h_a * 2 * dk_a
    tab = pl.BlockSpec((tm, LANES), lambda i: (0, 0))
    in_specs = [pl.BlockSpec((g_blk, r, d), lambda i: (i, 0, 0)),
                pl.BlockSpec((g_blk, 6, d), lambda i: (i, 0, 0)),
                tab, tab, tab, tab] + [_full(w) for w in weights]
    widths = [n_a, n_a, n_a, kv_lora, 32, h_b * kv_lora, h_b * 32]
    return pl.pallas_call(
        functools.partial(_proj_sample_kernel, cfg),
        grid=(ng // g_blk,),
        in_specs=in_specs,
        out_specs=[pl.BlockSpec((tm, w), lambda i: (i, 0)) for w in widths],
        out_shape=[jax.ShapeDtypeStruct((n, w), F32) for w in widths],
        compiler_params=_cparams(("parallel",)),
        name="proj_sample",
    )(x3, mod3, *tables, *weights)


def _prompt_attn_kernel(h_a, h_b, lam_init, tq, q_ref, kt_ref, v_ref, lamv_ref, oa_ref, ob_ref,
                        m_ref, l_ref, acc_ref):
    qi = pl.program_id(1)
    n_streams = 2 * h_a + h_b
    row = lax.broadcasted_iota(jnp.int32, (tq, tq), 0)
    col = lax.broadcasted_iota(jnp.int32, (tq, tq), 1)
    causal = col <= row

    def slabs(st):
        if st < 2 * h_a:
            return st // 2, st // 2
        h = st - 2 * h_a
        return h_a + h, h_a + h // 2

    m_ref[...] = jnp.full(m_ref.shape, NEG_INF, F32)
    l_ref[...] = jnp.zeros(l_ref.shape, F32)
    acc_ref[...] = jnp.zeros(acc_ref.shape, F32)

    def block(js, masked):
        for st in range(n_streams):
            ks, vs = slabs(st)
            q = q_ref[:, _slab(st)]
            ss = [_dot(q, kt_ref[j, _slab(ks), :]) for j in js]
            if masked:
                ss[-1] = jnp.where(causal, ss[-1], NEG_INF)
            s = jnp.concatenate(ss, axis=1)
            n_tiles = s.shape[1] // LANES
            m_old = m_ref[st]
            m_new = jnp.maximum(m_old, jnp.max(s, axis=-1, keepdims=True))
            alpha = jnp.exp2(m_old - m_new)
            p = jnp.exp2(s - jnp.concatenate([m_new] * n_tiles, axis=1))
            psum = p[:, 0:LANES]
            for c in range(1, n_tiles):
                psum = psum + p[:, _slab(c)]
            l_ref[st] = alpha * l_ref[st] + psum
            m_ref[st] = m_new
            pv = None
            for i, j in enumerate(js):
                k0 = pl.multiple_of(j * tq, tq)
                d = _dot(p[:, i * tq:(i + 1) * tq].astype(BF16), v_ref[pl.ds(k0, tq), _slab(vs)])
                pv = d if pv is None else pv + d
            acc_ref[st] = alpha * acc_ref[st] + pv

    def body(jj, carry):
        block([2 * jj], False)
        block([2 * jj + 1], False)
        return carry
    lax.fori_loop(0, qi // 2, body, 0)

    @pl.when(qi % 2 == 1)
    def _():
        block([qi - 1], False)
    block([qi], True)

    lam = _lambda(lamv_ref[...], lam_init)
    out = lambda st: acc_ref[st] / jnp.sum(l_ref[st], axis=-1, keepdims=True)
    for h in range(h_a):
        oa_ref[:, _slab(h)] = out(2 * h) - lam * out(2 * h + 1)
    low = lax.broadcasted_iota(jnp.int32, (tq, LANES), 1) < 64
    for hp in range(h_b // 2):
        ob_ref[:, _slab(hp)] = jnp.where(low, out(2 * h_a + 2 * hp), out(2 * h_a + 2 * hp + 1)).astype(BF16)


def _prompt_attn(h_a, h_b, lam_init, batch, seq, tq, q, kt, v, lamv):
    nq = seq // tq
    n = batch * seq
    n_streams = 2 * h_a + h_b
    qspec = lambda w: pl.BlockSpec((tq, w), lambda b, i: (b * nq + i, 0))
    return pl.pallas_call(
        functools.partial(_prompt_attn_kernel, h_a, h_b, lam_init, tq),
        grid=(batch, nq),
        in_specs=[qspec(q.shape[1]),
                  pl.BlockSpec((None,) + kt.shape[1:], lambda b, i: (b, 0, 0, 0)),
                  pl.BlockSpec((seq, v.shape[1]), lambda b, i: (b, 0)),
                  _full(lamv)],
        out_specs=[qspec(h_a * LANES), qspec(h_b * 64)],
        out_shape=[jax.ShapeDtypeStruct((n, h_a * LANES), F32), jax.ShapeDtypeStruct((n, h_b * 64), BF16)],
        scratch_shapes=[pltpu.VMEM((n_streams, tq, LANES), F32)] * 3,
        compiler_params=_cparams(("parallel", "arbitrary")),
        name="prompt_attn",
    )(q, kt, v, lamv)


def _sample_attn_kernel(h_a, h_b, kv_lora, lam_init, final, npg, base, pt_ref, qa_ref, qlat_ref, qrope_ref, kn_ref,
                        vn_ref, cn_ref, rn_ref, lamv_ref, wuv_ref,
                        x_ref, mod_ref, poa_ref, pob_ref, gsub_ref, wout_ref, w1_ref, w2_ref, gfin_ref,
                        ck_hbm, cv_hbm, cc_hbm, cr_hbm, oa_ref, ob_ref, y_ref,
                        qbd, ql, qr, m_a, l_a, acc_a, m_b, l_b, acc_b, kbuf, vbuf, cbuf, rbuf, sem,
                        x1_scr, hm_scr, ff_scr):
    i, j = pl.program_id(0), pl.program_id(1)
    steps = pl.num_programs(1)
    total = pl.num_programs(0) * steps
    n = i * steps + j
    slot = n % 2
    t = qa_ref.shape[0]
    page = cbuf.shape[2]
    rows_a = 2 * h_a * t
    rows_b = h_b * t
    caches, bufs = (ck_hbm, cv_hbm, cc_hbm, cr_hbm), (kbuf, vbuf, cbuf, rbuf)

    def page_copies(step, dst_slot):
        si, sj = step // steps, step % steps
        copies = []
        for k in range(npg):
            pid = base + pt_ref[si, sj * npg + k]
            for kind in range(4):
                copies.append(pltpu.make_async_copy(caches[kind].at[pid], bufs[kind].at[dst_slot, k],
                                                    sem.at[dst_slot, kind]))
        return copies

    @pl.when(n == 0)
    def _():
        for c in page_copies(0, 0):
            c.start()

        @pl.when(total > 1)
        def _():
            for c in page_copies(1, 1):
                c.start()

    def softmax_step(m_ref, l_ref, s):
        m_old = m_ref[...]
        m_new = jnp.maximum(m_old, jnp.max(s, axis=-1, keepdims=True))
        alpha = jnp.exp(m_old - m_new)
        p = jnp.exp(s - m_new)
        l_ref[...] = alpha * l_ref[...] + jnp.sum(p, axis=-1, keepdims=True)
        m_ref[...] = m_new
        return alpha, p

    def attend(sa, sb, n_chunks, val_a, val_b, between=None):
        extra = between() if between is not None else None
        alpha_a, pa = softmax_step(m_a, l_a, sa)
        alpha_b, pb = softmax_step(m_b, l_b, sb)
        pv = []
        for h in range(h_a):
            ph = pa[2 * t * h:2 * t * (h + 1)]
            acc = _dot(ph[:, 0:page].astype(BF16), val_a(h, 0))
            for i in range(1, n_chunks):
                acc = acc + _dot(ph[:, i * page:(i + 1) * page].astype(BF16), val_a(h, i))
            pv.append(acc)
        acc = _dot(pb[:, 0:page].astype(BF16), val_b(0))
        for i in range(1, n_chunks):
            acc = acc + _dot(pb[:, i * page:(i + 1) * page].astype(BF16), val_b(i))
        acc_a[...] = alpha_a * acc_a[...] + jnp.concatenate(pv, axis=0)
        acc_b[...] = alpha_b * acc_b[...] + acc
        return extra

    @pl.when(j == 0)
    def _():
        qa = qa_ref[...]
        r = lax.broadcasted_iota(jnp.int32, (rows_a, qa.shape[1]), 0)
        c = lax.broadcasted_iota(jnp.int32, (rows_a, qa.shape[1]), 1)
        dk = qa.shape[1] // (2 * h_a)
        qbd[...] = jnp.where((r >> _log2(t)) == (c >> _log2(dk)),
                             jnp.concatenate([qa] * (2 * h_a), axis=0), 0.0).astype(BF16)
        ql[...] = jnp.concatenate([qlat_ref[:, h * kv_lora:(h + 1) * kv_lora] for h in range(h_b)],
                                  axis=0).astype(BF16)
        qr[...] = jnp.concatenate([qrope_ref[:, h * 32:(h + 1) * 32] for h in range(h_b)], axis=0).astype(BF16)
        for m_ref, l_ref, acc_ref in ((m_a, l_a, acc_a), (m_b, l_b, acc_b)):
            m_ref[...] = jnp.full(m_ref.shape, NEG_INF, F32)
            l_ref[...] = jnp.zeros(l_ref.shape, F32)
            acc_ref[...] = jnp.zeros(acc_ref.shape, F32)
        pad = lambda ref: jnp.concatenate(
            [ref[...], jnp.zeros((page - t, ref.shape[1]), F32)], axis=0).astype(BF16)
        kn, vn, cn, rn = pad(kn_ref), pad(vn_ref), pad(cn_ref), pad(rn_ref)
        ca = lax.broadcasted_iota(jnp.int32, (rows_a, page), 1)
        ra = lax.broadcasted_iota(jnp.int32, (rows_a, page), 0)
        cb = lax.broadcasted_iota(jnp.int32, (rows_b, page), 1)
        rb = lax.broadcasted_iota(jnp.int32, (rows_b, page), 0)
        sa = jnp.where(ca <= (ra & (t - 1)), _dot_nt(qbd[...], kn), NEG_INF)
        sb = jnp.where(cb <= (rb & (t - 1)), _dot_nt(ql[...], cn) + _dot_nt(qr[...], rn), NEG_INF)
        attend(sa, sb, 1, lambda h, i: vn[:, _slab(h)], lambda i: cn)

    mod = mod_ref[...]

    @pl.when(j == 0)
    def _():
        x1, hm = _mix_in(h_a, 1.0 - lam_init, x_ref[...], mod, poa_ref, pob_ref, gsub_ref, wout_ref)
        x1_scr[...] = x1
        hm_scr[...] = hm
        ff_scr[...] = jnp.zeros(ff_scr.shape, F32)

    for c in page_copies(n, slot):
        c.wait()
    cs = [cbuf[slot, k].astype(BF16) for k in range(npg)]
    sa = jnp.concatenate([_dot(qbd[...], kbuf[slot, k].astype(BF16)) for k in range(npg)], axis=1)
    sb = jnp.concatenate([_dot_nt(ql[...], cs[k]) + _dot(qr[...], rbuf[slot, k].astype(BF16))
                          for k in range(npg)], axis=1)
    hid = attend(sa, sb, npg, lambda h, k: vbuf[slot, k, pl.ds(h, page, stride=h_a), :].astype(BF16),
                 lambda k: cs[k], between=lambda: _dot(hm_scr[...], w1_ref[j]))
    ff_scr[...] += _dot(jnp.square(jnp.maximum(hid, 0.0)).astype(BF16), w2_ref[j])

    @pl.when(j == pl.num_programs(1) - 1)
    def _():
        lam = _lambda(lamv_ref[...], lam_init)
        oa = acc_a[...] / l_a[...]
        for h in range(h_a):
            oa_ref[:, _slab(h)] = oa[(2 * h) * t:(2 * h + 1) * t] - lam * oa[(2 * h + 1) * t:(2 * h + 2) * t]
        olat = (acc_b[...] / l_b[...]).astype(BF16)
        ofull = _dot(olat, wuv_ref[...])
        cc = lax.broadcasted_iota(jnp.int32, (t, ofull.shape[1]), 1) >> 6
        ob = jnp.zeros((t, ofull.shape[1]), F32)
        for h in range(h_b):
            ob = jnp.where(cc == h, ofull[h * t:(h + 1) * t, :], ob)
        ob_ref[...] = ob

    @pl.when(n + 2 < total)
    def _():
        for c in page_copies(n + 2, slot):
            c.start()

    @pl.when(j == steps - 1)
    def _():
        y_ref[...] = _mix_out(final, x1_scr[...], mod, ff_scr[...], gfin_ref)


def _sample_attn(h_a, h_b, kv_lora, lam_init, final, base, page_table, qa3, qlat3, qrope3, kn3, vn3, cn3, rn3,
                 lamv, wuv, cache_kt, cache_v, cache_c, cache_rt, xp3, mod_p, poa, pob, gsub, wout, w1, w2, gfin):
    s, t, n_a = qa3.shape
    n_pages = page_table.shape[1]
    steps = w1.shape[0]
    npg = n_pages // steps
    n_tiles, tm, d = xp3.shape
    assert n_tiles == s and n_pages % steps == 0
    tiles_per_mod = n_tiles // mod_p.shape[0]
    seq = lambda a: pl.BlockSpec((None,) + a.shape[1:], lambda i, j, pt: (i, 0, 0))
    once = lambda a: pl.BlockSpec(a.shape, lambda i, j, pt, nd=a.ndim: (0,) * nd, pipeline_mode=pl.Buffered(1))
    caches = (cache_kt, cache_v, cache_c, cache_rt)
    in_specs = [seq(qa3), seq(qlat3), seq(qrope3), seq(kn3), seq(vn3), seq(cn3), seq(rn3), _full(lamv), _full(wuv)]
    in_specs += [pl.BlockSpec((1, tm, d), lambda i, j, pt: (i, 0, 0)),
                 pl.BlockSpec((1, 6, d), lambda i, j, pt: (i // tiles_per_mod, 0, 0)),
                 pl.BlockSpec((tm, poa.shape[1]), lambda i, j, pt: (i, 0)),
                 pl.BlockSpec((tm, pob.shape[1]), lambda i, j, pt: (i, 0)),
                 _full(gsub), once(wout), once(w1), once(w2), _full(gfin)]
    in_specs += [pl.BlockSpec(memory_space=pl.ANY)] * len(caches)
    rows_a, rows_b = 2 * h_a * t, h_b * t
    dv_a = cache_v.shape[2]
    scratch = [pltpu.VMEM((rows_a, n_a), BF16), pltpu.VMEM((rows_b, kv_lora), BF16), pltpu.VMEM((rows_b, 32), BF16),
               pltpu.VMEM((rows_a, 1), F32), pltpu.VMEM((rows_a, 1), F32), pltpu.VMEM((rows_a, dv_a), F32),
               pltpu.VMEM((rows_b, 1), F32), pltpu.VMEM((rows_b, 1), F32), pltpu.VMEM((rows_b, kv_lora), F32)]
    scratch += [pltpu.VMEM((2, npg) + a.shape[1:], F32) for a in caches]
    scratch += [pltpu.SemaphoreType.DMA((2, len(caches)))]
    scratch += [pltpu.VMEM((1, tm, d), F32), pltpu.VMEM((tm, d), BF16), pltpu.VMEM((tm, d), F32)]
    out_wa, out_wb = h_a * dv_a, wuv.shape[1]
    grid_spec = pltpu.PrefetchScalarGridSpec(
        num_scalar_prefetch=1,
        grid=(s, steps),
        in_specs=in_specs,
        out_specs=[pl.BlockSpec((None, t, out_wa), lambda i, j, pt: (i, 0, 0)),
                   pl.BlockSpec((None, t, out_wb), lambda i, j, pt: (i, 0, 0)),
                   pl.BlockSpec((1, tm, d), lambda i, j, pt: (i, 0, 0))],
        scratch_shapes=scratch)
    return pl.pallas_call(
        functools.partial(_sample_attn_kernel, h_a, h_b, kv_lora, lam_init, final, npg, base),
        grid_spec=grid_spec,
        out_shape=[jax.ShapeDtypeStruct((s, t, out_wa), F32), jax.ShapeDtypeStruct((s, t, out_wb), F32),
                   jax.ShapeDtypeStruct(xp3.shape, F32)],
        compiler_params=_cparams(("arbitrary", "arbitrary")),
        name="sample_attn",
    )(page_table, qa3, qlat3, qrope3, kn3, vn3, cn3, rn3, lamv, wuv, xp3, mod_p, poa, pob, gsub, wout, w1, w2,
      gfin, *caches)


def _mix_ffn_kernel(h_a, out_scale, final, x_ref, mod_ref, oa_ref, ob_ref, gsub_ref, wout_ref,
                    w1_ref, w2_ref, gfin_ref, y_ref):
    mod = mod_ref[...]
    x1, hm = _mix_in(h_a, out_scale, x_ref[...], mod, oa_ref, ob_ref, gsub_ref, wout_ref)
    ff = _ffn_chunk(hm, w1_ref[0], w2_ref[0])
    for c in range(1, w1_ref.shape[0]):
        ff = ff + _ffn_chunk(hm, w1_ref[c], w2_ref[c])
    y_ref[...] = _mix_out(final, x1, mod, ff, gfin_ref)


def _mix_in(h_a, out_scale, x, mod, oa_ref, ob_ref, gsub_ref, wout_ref):
    g, r, d = x.shape
    gsub = gsub_ref[...] * out_scale
    parts = [(_rms(oa_ref[:, _slab(h)]) * gsub).astype(BF16) for h in range(h_a)]
    parts.append(ob_ref[...].astype(BF16))
    att = _dot(jnp.concatenate(parts, axis=1), wout_ref[...])
    x1 = x + mod[:, 2:3, :] * att.reshape(g, r, d)
    hm = (_rms(x1) * (1.0 + mod[:, 4:5, :]) + mod[:, 3:4, :]).reshape(g * r, d).astype(BF16)
    return x1, hm


def _ffn_chunk(hm, w1c, w2c):
    a = _dot(hm, w1c)
    return _dot(jnp.square(jnp.maximum(a, 0.0)).astype(BF16), w2c)


def _mix_out(final, x1, mod, ff, gfin_ref):
    x2 = x1 + mod[:, 5:6, :] * ff.reshape(x1.shape)
    if final:
        x2 = _rms(x2) * gfin_ref[...]
    return x2


def _mix_ffn(h_a, out_scale, final, x3, mod3, oa, ob, gsub, wout, w1, w2, gfin, g_blk, tiles_per_mod):
    ng, r, d = x3.shape
    tm = g_blk * r
    row = lambda a: pl.BlockSpec((tm, a.shape[1]), lambda i: (i, 0))
    return pl.pallas_call(
        functools.partial(_mix_ffn_kernel, h_a, out_scale, final),
        grid=(ng // g_blk,),
        in_specs=[pl.BlockSpec((g_blk, r, d), lambda i: (i, 0, 0)),
                  pl.BlockSpec((g_blk, 6, d), lambda i: (i // tiles_per_mod, 0, 0)),
                  row(oa), row(ob), _full(gsub), _full(wout), _full(w1), _full(w2), _full(gfin)],
        out_specs=pl.BlockSpec((g_blk, r, d), lambda i: (i, 0, 0)),
        out_shape=jax.ShapeDtypeStruct((ng, r, d), F32),
        compiler_params=_cparams(("parallel",)),
        name="mix_ffn",
    )(x3, mod3, oa, ob, gsub, wout, w1, w2, gfin)


def _rope_angles(pos, d):
    inv = 1.0 / (ROPE_THETA ** (jnp.arange(0, d, 2, dtype=F32) / d))
    ang = pos.astype(F32)[:, None] * inv[None, :]
    return jnp.cos(ang), jnp.sin(ang)


def _rope_tables(pos, d_a, d_b):
    (ca, sa), (cb, sb) = _rope_angles(pos, d_a), _rope_angles(pos, d_b)
    n = pos.shape[0]
    reps = LANES // d_a
    cosa = jnp.tile(jnp.concatenate([ca, ca], axis=1), (1, reps))
    sina = jnp.tile(jnp.concatenate([-sa, sa], axis=1), (1, reps))
    one, zero = jnp.ones((n, 64), F32), jnp.zeros((n, 64), F32)
    cosb = jnp.concatenate([one, cb, cb, one[:, :LANES - 64 - d_b]], axis=1)
    sinb = jnp.concatenate([zero, -sb, sb, zero[:, :LANES - 64 - d_b]], axis=1)
    return cosa, sina, cosb, sinb


def kernel(x_prompt, x_sample, c_prompt, c_sample, cache_diff_k, cache_diff_v, cache_mla_ckv, cache_mla_krope,
           page_table, w_ada, b_ada, w_in, g_q, w_uq, w_uk, g_kv, lambda_q1, lambda_k1, lambda_q2, lambda_k2,
           g_sub, w_uv, w_out, w_ff1, w_ff2, g_final):
    batch, seq, d = x_prompt.shape
    s_batch, s_seq, _ = x_sample.shape
    depth, n_phys, page, h_a, _, dk_a = cache_diff_k.shape
    dv_a = cache_diff_v.shape[-1]
    kv_lora = cache_mla_ckv.shape[-1]
    qk_rope = cache_mla_krope.shape[-1]
    h_b, _, qk_nope = w_uk.shape[1:]
    v_head = w_uv.shape[-1]
    q_lora = g_q.shape[-1]
    past_len = page_table.shape[1] * page
    n_a = h_a * 2 * dk_a
    assert (dk_a, dv_a, qk_nope, qk_rope, v_head) == (64, 128, 64, 32, 64), "head geometry the lane layouts assume"
    assert w_in.shape[-1] == 3 * n_a + q_lora + kv_lora + qk_rope
    cfg = (h_a, dk_a, h_b, q_lora, kv_lora, dk_a ** -0.5, (qk_nope + qk_rope) ** -0.5)

    tm_p, ts, ff_chunk = batch * seq // s_batch, 32, 1024
    assert tm_p * s_batch == batch * seq and seq % tm_p == 0 and tm_p % LANES == 0
    pos_p = jnp.arange(seq)
    (ca_p, sa_p), (cb_p, sb_p) = _rope_angles(pos_p, dk_a), _rope_angles(pos_p, qk_rope)
    tab_p = _rope_tables(pos_p, dk_a, qk_rope) + (ca_p.T, sa_p.T, cb_p.T, sb_p.T)
    tab_s = tuple(jnp.tile(a, (ts, 1)) for a in _rope_tables(past_len + jnp.arange(s_seq), dk_a, qk_rope))

    xp3 = x_prompt.reshape(batch * seq // tm_p, tm_p, d)
    xs3 = x_sample
    c_all = jnp.concatenate([c_prompt, c_sample], axis=0)
    ckt = jnp.transpose(cache_diff_k, (0, 1, 3, 4, 5, 2)).reshape(depth * n_phys, n_a, page)
    cv = cache_diff_v.reshape(depth * n_phys, page * h_a, dv_a)
    cc = cache_mla_ckv.reshape(depth * n_phys, page, kv_lora)
    crt = jnp.transpose(cache_mla_krope, (0, 1, 3, 2)).reshape(depth * n_phys, qk_rope, page)

    sel_np = np.zeros((h_b * LANES, h_b * qk_rope), np.float32)
    sel_rows = (np.arange(h_b)[:, None] * LANES + 64 + np.arange(qk_rope)[None, :]).reshape(-1)
    sel_np[sel_rows, np.arange(h_b * qk_rope)] = 1.0
    sel = jnp.asarray(sel_np, dtype=BF16)

    new_p, new_s = [], []
    for l in range(depth):
        lam_init = 0.8 - 0.6 * math.exp(-0.3 * l)
        mod = _adaln(c_all, w_ada[l], b_ada[l]).reshape(batch + s_batch, 6, d)
        mod_p, mod_s = mod[:batch], mod[batch:]
        off_ckv = 3 * n_a + q_lora
        off_kr = off_ckv + kv_lora
        w_l = w_in[l]
        win_s = jnp.concatenate([w_l[:, :off_kr], jnp.zeros((d, 64), F32), w_l[:, off_kr:],
                                 jnp.zeros((d, LANES - 64 - qk_rope), F32)], axis=1).astype(BF16)
        wmain = jnp.concatenate([w_l[:, :n_a], w_l[:, 2 * n_a:off_kr]], axis=1).astype(BF16)
        wkt = jnp.concatenate([w_l[:, n_a:2 * n_a], w_l[:, off_kr:]], axis=1).T.astype(BF16)
        wuq = jnp.pad(w_uq[l].reshape(q_lora, h_b, qk_nope + qk_rope),
                      ((0, 0), (0, 0), (0, LANES - qk_nope - qk_rope))).reshape(q_lora, h_b * LANES).astype(BF16)
        wukt = jnp.transpose(w_uk[l], (0, 2, 1)).reshape(h_b * qk_nope, kv_lora).astype(BF16)
        wuv_all = jnp.transpose(w_uv[l], (1, 0, 2)).reshape(kv_lora, h_b * v_head).astype(BF16)
        wabs = jnp.pad(jnp.transpose(w_uk[l], (0, 2, 1)), ((0, 0), (0, LANES - qk_nope), (0, 0))).astype(BF16)
        gq, gkv, gsub = g_q[l].reshape(1, -1), g_kv[l].reshape(1, -1), g_sub[l].reshape(1, -1)
        lamv = jnp.stack([lambda_q1[l], lambda_k1[l], lambda_q2[l], lambda_k2[l]], axis=0)
        d_ff = w_ff1.shape[-1]
        n_ch = d_ff // ff_chunk
        wout = w_out[l].astype(BF16)
        w1 = jnp.transpose(w_ff1[l].reshape(d, n_ch, ff_chunk), (1, 0, 2)).astype(BF16)
        w2 = w_ff2[l].reshape(n_ch, ff_chunk, d).astype(BF16)
        gfin = g_final.reshape(1, d)
        final = l == depth - 1

        q, kat, krt, kt, v, va, ckv = _proj_prompt(
            cfg, xp3, mod_p, tab_p, (wmain, wkt, gq, wuq, gkv, wukt, wuv_all), batch, seq)
        oa, ob = _prompt_attn(h_a, h_b, lam_init, batch, seq, tm_p, q, kt, v, lamv)
        new_p.append((jnp.transpose(kat.reshape(batch, h_a, 2, dk_a, seq), (0, 4, 1, 2, 3)),
                      va.reshape(batch, seq, h_a, dv_a),
                      ckv.reshape(batch, seq, kv_lora),
                      jnp.transpose(krt, (0, 2, 1))))

        sqa, ska, sva, sckv, skr, sqlat, sqrope = _proj_sample(
            cfg, xs3, mod_s, tab_s, (win_s, gq, wuq, gkv, wabs, sel), ts)
        r3 = lambda a: a.reshape(s_batch, s_seq, a.shape[-1])
        soa, sob, xp3 = _sample_attn(h_a, h_b, kv_lora, lam_init, final, l * n_phys, page_table, r3(sqa), r3(sqlat),
                                     r3(sqrope), r3(ska), r3(sva), r3(sckv), r3(skr), lamv, wuv_all, ckt, cv, cc, crt,
                                     xp3, mod_p, oa, ob, gsub, wout, w1, w2, gfin)
        n_s = s_batch * s_seq
        xs3 = _mix_ffn(h_a, 1.0 - lam_init, final, xs3, mod_s, soa.reshape(n_s, -1), sob.reshape(n_s, -1),
                       gsub, wout, w1, w2, gfin, ts, 1)
        new_s.append((ska.reshape(s_batch, s_seq, h_a, 2, dk_a), sva.reshape(s_batch, s_seq, h_a, dv_a),
                      sckv.reshape(s_batch, s_seq, kv_lora), skr.reshape(s_batch, s_seq, qk_rope)))

    y_prompt = xp3.reshape(batch, seq, d)
    y_sample = xs3
    stack = lambda rows, i: jnp.stack([r[i] for r in rows], axis=0)
    return (y_prompt, y_sample, stack(new_p, 0), stack(new_p, 1), stack(new_p, 2), stack(new_p, 3),
            stack(new_s, 0), stack(new_s, 1), stack(new_s, 2), stack(new_s, 3))
```

```python
import functools
import math

import jax
import jax.numpy as jnp
import numpy as np
from jax import lax
from jax.experimental import pallas as pl
from jax.experimental.pallas import tpu as pltpu

F32 = jnp.float32
BF16 = jnp.bfloat16

ROPE_THETA = 10000.0
EPS = 1e-6
NEG_INF = -1e30
LOG2E = math.log2(math.e)
LANES = 128
VMEM_LIMIT = 56 * 1024 * 1024


def _cparams(sem):
    return pltpu.CompilerParams(dimension_semantics=sem, vmem_limit_bytes=VMEM_LIMIT)


def _rms(x):
    return x * lax.rsqrt(jnp.mean(x * x, axis=-1, keepdims=True) + EPS)


def _dot(a, b):
    return jnp.dot(a, b, preferred_element_type=F32)


def _dot_nt(a, b):
    return lax.dot_general(a, b, (((1,), (1,)), ((), ())), preferred_element_type=F32)


def _slab(i, w=LANES):
    return slice(i * w, (i + 1) * w)


def _log2(n):
    assert n & (n - 1) == 0, "power of two expected"
    return n.bit_length() - 1


def _rope(x, cos, sin_signed, half):
    lane = lax.broadcasted_iota(jnp.int32, x.shape, 1)
    first = (lane & (2 * half - 1)) < half
    swapped = jnp.where(first, pltpu.roll(x, LANES - half, 1), pltpu.roll(x, half, 1))
    return x * cos + swapped * sin_signed


def _rope_rows(x1, x2, cos, sin):
    return x1 * cos - x2 * sin, x2 * cos + x1 * sin


def _lambda(lamv, lam_init):
    e1 = jnp.exp(jnp.sum(lamv[0:1] * lamv[1:2], axis=-1, keepdims=True))
    e2 = jnp.exp(jnp.sum(lamv[2:3] * lamv[3:4], axis=-1, keepdims=True))
    return e1 - e2 + lam_init


def _full(a):
    return pl.BlockSpec(a.shape, lambda *idx, nd=a.ndim: (0,) * nd)


def _adaln_kernel(c_ref, w_ref, b_ref, o_ref):
    c = c_ref[...]
    s = c / (1.0 + jnp.exp(-c))
    o_ref[...] = _dot(s.astype(BF16), w_ref[...].astype(BF16)) + b_ref[...]


def _adaln(c_all, w_ada, b_ada):
    n, d = c_all.shape
    d_out = w_ada.shape[1]
    tn = 1024
    return pl.pallas_call(
        _adaln_kernel,
        grid=(d_out // tn,),
        in_specs=[pl.BlockSpec((n, d), lambda j: (0, 0)),
                  pl.BlockSpec((d, tn), lambda j: (0, j)),
                  pl.BlockSpec((1, tn), lambda j: (0, j))],
        out_specs=pl.BlockSpec((n, tn), lambda j: (0, j)),
        out_shape=jax.ShapeDtypeStruct((n, d_out), F32),
        compiler_params=_cparams(("arbitrary",)),
        name="adaln",
    )(c_all, w_ada, b_ada.reshape(1, d_out))


def _modulated(x_ref, mod_ref):
    x = x_ref[...]
    g, r, d = x.shape
    mod = mod_ref[...]
    h = _rms(x) * (1.0 + mod[:, 1:2, :]) + mod[:, 0:1, :]
    return h.reshape(g * r, d).astype(BF16)


def _proj_prompt_kernel(cfg, x_ref, mod_ref, cosa_ref, sina_ref, cosb_ref, sinb_ref, cosat_ref, sinat_ref,
                        cosbt_ref, sinbt_ref, wmain_ref, wkt_ref, gq_ref, wuq_ref, gkv_ref, wukt_ref, wuv_ref,
                        q_ref, kat_ref, krt_ref, kt_ref, v_ref, va_ref, ckv_ref):
    h_a, dk_a, h_b, q_lora, kv_lora, scale_a, scale_b = cfg
    scale_a, scale_b = scale_a * LOG2E, scale_b * LOG2E
    n_a = h_a * 2 * dk_a
    hmod = _modulated(x_ref, mod_ref)
    tm = hmod.shape[0]
    p = _dot(hmod, wmain_ref[...])
    kk = _dot_nt(wkt_ref[...], hmod)
    cosa, sina, cosb, sinb = cosa_ref[...], sina_ref[...], cosb_ref[...], sinb_ref[...]
    lane = lax.broadcasted_iota(jnp.int32, (tm, LANES), 1)
    low = lane < dk_a

    for h in range(h_a):
        q = _rope(p[:, _slab(h)], cosa, sina, dk_a // 2) * scale_a
        q_ref[:, _slab(2 * h)] = jnp.where(low, q, 0.0).astype(BF16)
        q_ref[:, _slab(2 * h + 1)] = jnp.where(low, 0.0, q).astype(BF16)
    cq = _rms(p[:, 2 * n_a:2 * n_a + q_lora]) * gq_ref[...]
    qb = _dot(cq.astype(BF16), wuq_ref[...])
    for h in range(h_b):
        q_ref[:, _slab(2 * h_a + h)] = (_rope(qb[:, _slab(h)], cosb, sinb, 16) * scale_b).astype(BF16)

    ca, sa = cosat_ref[...], sinat_ref[...]
    hd = dk_a // 2
    for g in range(n_a // dk_a):
        o1, o2 = _rope_rows(kk[g * dk_a:g * dk_a + hd], kk[g * dk_a + hd:(g + 1) * dk_a], ca, sa)
        kat_ref[g * dk_a:g * dk_a + hd, :] = o1
        kat_ref[g * dk_a + hd:(g + 1) * dk_a, :] = o2
        kt_ref[g * dk_a:g * dk_a + hd, :] = o1.astype(BF16)
        kt_ref[g * dk_a + hd:(g + 1) * dk_a, :] = o2.astype(BF16)
    r1, r2 = _rope_rows(kk[n_a:n_a + 16], kk[n_a + 16:n_a + 32], cosbt_ref[...], sinbt_ref[...])
    krt_ref[0:16, :] = r1
    krt_ref[16:32, :] = r2

    v = p[:, n_a:2 * n_a]
    for h in range(h_a):
        va_ref[pl.ds(h, tm, stride=h_a), :] = v[:, _slab(h)]
    v_ref[:, 0:n_a] = v.astype(BF16)
    ckv = _rms(p[:, 2 * n_a + q_lora:]) * gkv_ref[...]
    ckv_ref[...] = ckv
    ckv_bf = ckv.astype(BF16)
    v_ref[:, n_a:] = _dot(ckv_bf, wuv_ref[...]).astype(BF16)

    knt = _dot_nt(wukt_ref[...], ckv_bf)
    r1b, r2b = r1.astype(BF16), r2.astype(BF16)
    zero = jnp.zeros((32, tm), BF16)
    for h in range(h_b):
        base = n_a + h * LANES
        kt_ref[base:base + 64, :] = knt[h * 64:(h + 1) * 64].astype(BF16)
        kt_ref[base + 64:base + 80, :] = r1b
        kt_ref[base + 80:base + 96, :] = r2b
        kt_ref[base + 96:base + 128, :] = zero


def _proj_prompt(cfg, x3, mod3, tables, weights, batch, seq):
    ng, tm, d = x3.shape
    n = ng * tm
    tpb = seq // tm
    h_a, dk_a, h_b, q_lora, kv_lora, _, _ = cfg
    n_a = h_a * 2 * dk_a
    tab = pl.BlockSpec((tm, LANES), lambda i: (i % tpb, 0))
    tabt = lambda a: pl.BlockSpec((a.shape[0], tm), lambda i: (0, i % tpb))
    in_specs = [pl.BlockSpec((1, tm, d), lambda i: (i, 0, 0)),
                pl.BlockSpec((1, 6, d), lambda i: (i // tpb, 0, 0)),
                tab, tab, tab, tab] + [tabt(a) for a in tables[4:]] + [_full(w) for w in weights]
    row = lambda w: pl.BlockSpec((tm, w), lambda i: (i, 0))
    nq = (2 * h_a + h_b) * LANES
    nk = n_a + h_b * LANES
    nv = n_a + h_b * 64
    out_specs = [row(nq),
                 pl.BlockSpec((None, n_a, tm), lambda i: (i // tpb, 0, i % tpb)),
                 pl.BlockSpec((None, 32, tm), lambda i: (i // tpb, 0, i % tpb)),
                 pl.BlockSpec((None, None, nk, tm), lambda i: (i // tpb, i % tpb, 0, 0)),
                 row(nv),
                 pl.BlockSpec((tm * h_a, LANES), lambda i: (i, 0)),
                 row(kv_lora)]
    out_shape = [jax.ShapeDtypeStruct((n, nq), BF16),
                 jax.ShapeDtypeStruct((batch, n_a, seq), F32),
                 jax.ShapeDtypeStruct((batch, 32, seq), F32),
                 jax.ShapeDtypeStruct((batch, tpb, nk, tm), BF16),
                 jax.ShapeDtypeStruct((n, nv), BF16),
                 jax.ShapeDtypeStruct((n * h_a, LANES), F32),
                 jax.ShapeDtypeStruct((n, kv_lora), F32)]
    return pl.pallas_call(
        functools.partial(_proj_prompt_kernel, cfg),
        grid=(ng,),
        in_specs=in_specs,
        out_specs=out_specs,
        out_shape=out_shape,
        compiler_params=_cparams(("parallel",)),
        name="proj_prompt",
    )(x3, mod3, *tables, *weights)


def _proj_sample_kernel(cfg, x_ref, mod_ref, cosa_ref, sina_ref, cosb_ref, sinb_ref, win_ref, gq_ref, wuq_ref,
                        gkv_ref, wabs_ref, sel_ref, qa_ref, ka_ref, va_ref, ckv_ref, kr_ref, qlat_ref, qrope_ref):
    h_a, dk_a, h_b, q_lora, kv_lora, scale_a, scale_b = cfg
    n_a = h_a * 2 * dk_a
    off_cq = 3 * n_a
    off_ckv = off_cq + q_lora
    off_kr = off_ckv + kv_lora
    hmod = _modulated(x_ref, mod_ref)
    p = _dot(hmod, win_ref[...])
    cosa, sina, cosb, sinb = cosa_ref[...], sina_ref[...], cosb_ref[...], sinb_ref[...]
    for s in range(n_a // LANES):
        qa_ref[:, _slab(s)] = _rope(p[:, _slab(s)], cosa, sina, dk_a // 2) * scale_a
        ka_ref[:, _slab(s)] = _rope(p[:, n_a + s * LANES:n_a + (s + 1) * LANES], cosa, sina, dk_a // 2)
    va_ref[...] = p[:, 2 * n_a:3 * n_a]
    cq = _rms(p[:, off_cq:off_ckv]) * gq_ref[...]
    qb = _dot(cq.astype(BF16), wuq_ref[...])
    ckv_ref[...] = _rms(p[:, off_ckv:off_kr]) * gkv_ref[...]
    kr_ref[...] = _rope(p[:, off_kr:off_kr + LANES], cosb, sinb, 16)[:, 64:96]
    qbr = [(_rope(qb[:, _slab(h)], cosb, sinb, 16) * scale_b).astype(BF16) for h in range(h_b)]
    for h in range(h_b):
        qlat_ref[:, h * kv_lora:(h + 1) * kv_lora] = _dot(qbr[h], wabs_ref[h])
    qrope_ref[...] = _dot(jnp.concatenate(qbr, axis=1), sel_ref[...])


def _proj_sample(cfg, x3, mod3, tables, weights, g_blk):
    ng, r, d = x3.shape
    tm = g_blk * r
    n = ng * r
    h_a, dk_a, h_b, q_lora, kv_lora, _, _ = cfg
    n_a = h_a * 2 * dk_a
    tab = pl.BlockSpec((tm, LANES), lambda i: (0, 0))
    in_specs = [pl.BlockSpec((g_blk, r, d), lambda i: (i, 0, 0)),
                pl.BlockSpec((g_blk, 6, d), lambda i: (i, 0, 0)),
                tab, tab, tab, tab] + [_full(w) for w in weights]
    widths = [n_a, n_a, n_a, kv_lora, 32, h_b * kv_lora, h_b * 32]
    return pl.pallas_call(
        functools.partial(_proj_sample_kernel, cfg),
        grid=(ng // g_blk,),
        in_specs=in_specs,
        out_specs=[pl.BlockSpec((tm, w), lambda i: (i, 0)) for w in widths],
        out_shape=[jax.ShapeDtypeStruct((n, w), F32) for w in widths],
        compiler_params=_cparams(("parallel",)),
        name="proj_sample",
    )(x3, mod3, *tables, *weights)


def _prompt_attn_kernel(h_a, h_b, lam_init, tq, q_ref, kt_ref, v_ref, lamv_ref, oa_ref, ob_ref,
                        m_ref, l_ref, acc_ref):
    qi = pl.program_id(1)
    n_streams = 2 * h_a + h_b
    row = lax.broadcasted_iota(jnp.int32, (tq, tq), 0)
    col = lax.broadcasted_iota(jnp.int32, (tq, tq), 1)
    causal = col <= row

    def slabs(st):
        if st < 2 * h_a:
            return st // 2, st // 2
        h = st - 2 * h_a
        return h_a + h, h_a + h // 2

    m_ref[...] = jnp.full(m_ref.shape, NEG_INF, F32)
    l_ref[...] = jnp.zeros(l_ref.shape, F32)
    acc_ref[...] = jnp.zeros(acc_ref.shape, F32)

    def block(js, masked):
        for st in range(n_streams):
            ks, vs = slabs(st)
            q = q_ref[:, _slab(st)]
            ss = [_dot(q, kt_ref[j, _slab(ks), :]) for j in js]
            if masked:
                ss[-1] = jnp.where(causal, ss[-1], NEG_INF)
            s = jnp.concatenate(ss, axis=1)
            n_tiles = s.shape[1] // LANES
            m_old = m_ref[st]
            m_new = jnp.maximum(m_old, jnp.max(s, axis=-1, keepdims=True))
            alpha = jnp.exp2(m_old - m_new)
            p = jnp.exp2(s - jnp.concatenate([m_new] * n_tiles, axis=1))
            psum = p[:, 0:LANES]
            for c in range(1, n_tiles):
                psum = psum + p[:, _slab(c)]
            l_ref[st] = alpha * l_ref[st] + psum
            m_ref[st] = m_new
            pv = None
            for i, j in enumerate(js):
                k0 = pl.multiple_of(j * tq, tq)
                d = _dot(p[:, i * tq:(i + 1) * tq].astype(BF16), v_ref[pl.ds(k0, tq), _slab(vs)])
                pv = d if pv is None else pv + d
            acc_ref[st] = alpha * acc_ref[st] + pv

    def body(jj, carry):
        block([2 * jj], False)
        block([2 * jj + 1], False)
        return carry
    lax.fori_loop(0, qi // 2, body, 0)

    @pl.when(qi % 2 == 1)
    def _():
        block([qi - 1], False)
    block([qi], True)

    lam = _lambda(lamv_ref[...], lam_init)
    out = lambda st: acc_ref[st] / jnp.sum(l_ref[st], axis=-1, keepdims=True)
    for h in range(h_a):
        oa_ref[:, _slab(h)] = out(2 * h) - lam * out(2 * h + 1)
    low = lax.broadcasted_iota(jnp.int32, (tq, LANES), 1) < 64
    for hp in range(h_b // 2):
        ob_ref[:, _slab(hp)] = jnp.where(low, out(2 * h_a + 2 * hp), out(2 * h_a + 2 * hp + 1)).astype(BF16)


def _prompt_attn(h_a, h_b, lam_init, batch, seq, tq, q, kt, v, lamv):
    nq = seq // tq
    n = batch * seq
    n_streams = 2 * h_a + h_b
    qspec = lambda w: pl.BlockSpec((tq, w), lambda b, i: (b * nq + i, 0))
    return pl.pallas_call(
        functools.partial(_prompt_attn_kernel, h_a, h_b, lam_init, tq),
        grid=(batch, nq),
        in_specs=[qspec(q.shape[1]),
                  pl.BlockSpec((None,) + kt.shape[1:], lambda b, i: (b, 0, 0, 0)),
                  pl.BlockSpec((seq, v.shape[1]), lambda b, i: (b, 0)),
                  _full(lamv)],
        out_specs=[qspec(h_a * LANES), qspec(h_b * 64)],
        out_shape=[jax.ShapeDtypeStruct((n, h_a * LANES), F32), jax.ShapeDtypeStruct((n, h_b * 64), BF16)],
        scratch_shapes=[pltpu.VMEM((n_streams, tq, LANES), F32)] * 3,
        compiler_params=_cparams(("parallel", "arbitrary")),
        name="prompt_attn",
    )(q, kt, v, lamv)


def _sample_attn_kernel(h_a, h_b, kv_lora, lam_init, final, npg, base, pt_ref, qa_ref, qlat_ref, qrope_ref, kn_ref,
                        vn_ref, cn_ref, rn_ref, lamv_ref, wuv_ref,
                        x_ref, mod_ref, poa_ref, pob_ref, gsub_ref, wout_ref, w1_ref, w2_ref, gfin_ref,
                        ck_hbm, cv_hbm, cc_hbm, cr_hbm, oa_ref, ob_ref, y_ref,
                        qbd, ql, qr, m_a, l_a, acc_a, m_b, l_b, acc_b, kbuf, vbuf, cbuf, rbuf, sem,
                        x1_scr, hm_scr, ff_scr):
    i, j = pl.program_id(0), pl.program_id(1)
    steps = pl.num_programs(1)
    total = pl.num_programs(0) * steps
    n = i * steps + j
    slot = n % 2
    t = qa_ref.shape[0]
    page = cbuf.shape[2]
    rows_a = 2 * h_a * t
    rows_b = h_b * t
    caches, bufs = (ck_hbm, cv_hbm, cc_hbm, cr_hbm), (kbuf, vbuf, cbuf, rbuf)

    def page_copies(step, dst_slot):
        si, sj = step // steps, step % steps
        copies = []
        for k in range(npg):
            pid = base + pt_ref[si, sj * npg + k]
            for kind in range(4):
                copies.append(pltpu.make_async_copy(caches[kind].at[pid], bufs[kind].at[dst_slot, k],
                                                    sem.at[dst_slot, kind]))
        return copies

    def fetch(step, dst_slot):
        for idx, c in enumerate(page_copies(step, dst_slot)):
            k, kind = divmod(idx, len(caches))
            c.start(priority=(k + kind) % 2)

    @pl.when(n == 0)
    def _():
        fetch(0, 0)

        @pl.when(total > 1)
        def _():
            fetch(1, 1)

    def softmax_step(m_ref, l_ref, s):
        m_old = m_ref[...]
        m_new = jnp.maximum(m_old, jnp.max(s, axis=-1, keepdims=True))
        alpha = jnp.exp(m_old - m_new)
        p = jnp.exp(s - m_new)
        l_ref[...] = alpha * l_ref[...] + jnp.sum(p, axis=-1, keepdims=True)
        m_ref[...] = m_new
        return alpha, p

    def attend(sa, sb, n_chunks, val_a, val_b):
        alpha, p = softmax_step(m_a, l_a, sa)
        pv = []
        for h in range(h_a):
            ph = p[2 * t * h:2 * t * (h + 1)]
            acc = _dot(ph[:, 0:page].astype(BF16), val_a(h, 0))
            for i in range(1, n_chunks):
                acc = acc + _dot(ph[:, i * page:(i + 1) * page].astype(BF16), val_a(h, i))
            pv.append(acc)
        acc_a[...] = alpha * acc_a[...] + jnp.concatenate(pv, axis=0)
        alpha, p = softmax_step(m_b, l_b, sb)
        acc = _dot(p[:, 0:page].astype(BF16), val_b(0))
        for i in range(1, n_chunks):
            acc = acc + _dot(p[:, i * page:(i + 1) * page].astype(BF16), val_b(i))
        acc_b[...] = alpha * acc_b[...] + acc

    @pl.when(j == 0)
    def _():
        qa = qa_ref[...]
        r = lax.broadcasted_iota(jnp.int32, (rows_a, qa.shape[1]), 0)
        c = lax.broadcasted_iota(jnp.int32, (rows_a, qa.shape[1]), 1)
        dk = qa.shape[1] // (2 * h_a)
        qbd[...] = jnp.where((r >> _log2(t)) == (c >> _log2(dk)),
                             jnp.concatenate([qa] * (2 * h_a), axis=0), 0.0).astype(BF16)
        ql[...] = jnp.concatenate([qlat_ref[:, h * kv_lora:(h + 1) * kv_lora] for h in range(h_b)],
                                  axis=0).astype(BF16)
        qr[...] = jnp.concatenate([qrope_ref[:, h * 32:(h + 1) * 32] for h in range(h_b)], axis=0).astype(BF16)
        for m_ref, l_ref, acc_ref in ((m_a, l_a, acc_a), (m_b, l_b, acc_b)):
            m_ref[...] = jnp.full(m_ref.shape, NEG_INF, F32)
            l_ref[...] = jnp.zeros(l_ref.shape, F32)
            acc_ref[...] = jnp.zeros(acc_ref.shape, F32)
        pad = lambda ref: jnp.concatenate(
            [ref[...], jnp.zeros((page - t, ref.shape[1]), F32)], axis=0).astype(BF16)
        kn, vn, cn, rn = pad(kn_ref), pad(vn_ref), pad(cn_ref), pad(rn_ref)
        ca = lax.broadcasted_iota(jnp.int32, (rows_a, page), 1)
        ra = lax.broadcasted_iota(jnp.int32, (rows_a, page), 0)
        cb = lax.broadcasted_iota(jnp.int32, (rows_b, page), 1)
        rb = lax.broadcasted_iota(jnp.int32, (rows_b, page), 0)
        sa = jnp.where(ca <= (ra & (t - 1)), _dot_nt(qbd[...], kn), NEG_INF)
        sb = jnp.where(cb <= (rb & (t - 1)), _dot_nt(ql[...], cn) + _dot_nt(qr[...], rn), NEG_INF)
        attend(sa, sb, 1, lambda h, i: vn[:, _slab(h)], lambda i: cn)

    for c in page_copies(n, slot):
        c.wait()
    cs = [cbuf[slot, k].astype(BF16) for k in range(npg)]
    sa = jnp.concatenate([_dot(qbd[...], kbuf[slot, k].astype(BF16)) for k in range(npg)], axis=1)
    sb = jnp.concatenate([_dot_nt(ql[...], cs[k]) + _dot(qr[...], rbuf[slot, k].astype(BF16))
                          for k in range(npg)], axis=1)
    attend(sa, sb, npg, lambda h, k: vbuf[slot, k, pl.ds(h, page, stride=h_a), :].astype(BF16), lambda k: cs[k])

    @pl.when(n + 2 < total)
    def _():
        fetch(n + 2, slot)

    @pl.when(j == pl.num_programs(1) - 1)
    def _():
        lam = _lambda(lamv_ref[...], lam_init)
        oa = acc_a[...] / l_a[...]
        for h in range(h_a):
            oa_ref[:, _slab(h)] = oa[(2 * h) * t:(2 * h + 1) * t] - lam * oa[(2 * h + 1) * t:(2 * h + 2) * t]
        olat = (acc_b[...] / l_b[...]).astype(BF16)
        ofull = _dot(olat, wuv_ref[...])
        cc = lax.broadcasted_iota(jnp.int32, (t, ofull.shape[1]), 1) >> 6
        ob = jnp.zeros((t, ofull.shape[1]), F32)
        for h in range(h_b):
            ob = jnp.where(cc == h, ofull[h * t:(h + 1) * t, :], ob)
        ob_ref[...] = ob

    mod = mod_ref[...]

    @pl.when(j == 0)
    def _():
        x1, hm = _mix_in(h_a, 1.0 - lam_init, x_ref[...], mod, poa_ref, pob_ref, gsub_ref, wout_ref)
        x1_scr[...] = x1
        hm_scr[...] = hm
        ff_scr[...] = jnp.zeros(ff_scr.shape, F32)
    ff_scr[...] += _ffn_chunk(hm_scr[...], w1_ref[j], w2_ref[j])

    @pl.when(j == steps - 1)
    def _():
        y_ref[...] = _mix_out(final, x1_scr[...], mod, ff_scr[...], gfin_ref)


def _sample_attn(h_a, h_b, kv_lora, lam_init, final, base, page_table, qa3, qlat3, qrope3, kn3, vn3, cn3, rn3,
                 lamv, wuv, cache_kt, cache_v, cache_c, cache_rt, xp3, mod_p, poa, pob, gsub, wout, w1, w2, gfin):
    s, t, n_a = qa3.shape
    n_pages = page_table.shape[1]
    steps = w1.shape[0]
    npg = n_pages // steps
    n_tiles, tm, d = xp3.shape
    assert n_tiles == s and n_pages % steps == 0
    tiles_per_mod = n_tiles // mod_p.shape[0]
    seq = lambda a: pl.BlockSpec((None,) + a.shape[1:], lambda i, j, pt: (i, 0, 0))
    once = lambda a: pl.BlockSpec(a.shape, lambda i, j, pt, nd=a.ndim: (0,) * nd, pipeline_mode=pl.Buffered(1))
    caches = (cache_kt, cache_v, cache_c, cache_rt)
    in_specs = [seq(qa3), seq(qlat3), seq(qrope3), seq(kn3), seq(vn3), seq(cn3), seq(rn3), _full(lamv), _full(wuv)]
    in_specs += [pl.BlockSpec((1, tm, d), lambda i, j, pt: (i, 0, 0)),
                 pl.BlockSpec((1, 6, d), lambda i, j, pt: (i // tiles_per_mod, 0, 0)),
                 pl.BlockSpec((tm, poa.shape[1]), lambda i, j, pt: (i, 0)),
                 pl.BlockSpec((tm, pob.shape[1]), lambda i, j, pt: (i, 0)),
                 _full(gsub), once(wout), once(w1), once(w2), _full(gfin)]
    in_specs += [pl.BlockSpec(memory_space=pl.ANY)] * len(caches)
    rows_a, rows_b = 2 * h_a * t, h_b * t
    dv_a = cache_v.shape[2]
    scratch = [pltpu.VMEM((rows_a, n_a), BF16), pltpu.VMEM((rows_b, kv_lora), BF16), pltpu.VMEM((rows_b, 32), BF16),
               pltpu.VMEM((rows_a, 1), F32), pltpu.VMEM((rows_a, 1), F32), pltpu.VMEM((rows_a, dv_a), F32),
               pltpu.VMEM((rows_b, 1), F32), pltpu.VMEM((rows_b, 1), F32), pltpu.VMEM((rows_b, kv_lora), F32)]
    scratch += [pltpu.VMEM((2, npg) + a.shape[1:], F32) for a in caches]
    scratch += [pltpu.SemaphoreType.DMA((2, len(caches)))]
    scratch += [pltpu.VMEM((1, tm, d), F32), pltpu.VMEM((tm, d), BF16), pltpu.VMEM((tm, d), F32)]
    out_wa, out_wb = h_a * dv_a, wuv.shape[1]
    grid_spec = pltpu.PrefetchScalarGridSpec(
        num_scalar_prefetch=1,
        grid=(s, steps),
        in_specs=in_specs,
        out_specs=[pl.BlockSpec((None, t, out_wa), lambda i, j, pt: (i, 0, 0)),
                   pl.BlockSpec((None, t, out_wb), lambda i, j, pt: (i, 0, 0)),
                   pl.BlockSpec((1, tm, d), lambda i, j, pt: (i, 0, 0))],
        scratch_shapes=scratch)
    return pl.pallas_call(
        functools.partial(_sample_attn_kernel, h_a, h_b, kv_lora, lam_init, final, npg, base),
        grid_spec=grid_spec,
        out_shape=[jax.ShapeDtypeStruct((s, t, out_wa), F32), jax.ShapeDtypeStruct((s, t, out_wb), F32),
                   jax.ShapeDtypeStruct(xp3.shape, F32)],
        compiler_params=_cparams(("arbitrary", "arbitrary")),
        name="sample_attn",
    )(page_table, qa3, qlat3, qrope3, kn3, vn3, cn3, rn3, lamv, wuv, xp3, mod_p, poa, pob, gsub, wout, w1, w2,
      gfin, *caches)


def _mix_ffn_kernel(h_a, out_scale, final, x_ref, mod_ref, oa_ref, ob_ref, gsub_ref, wout_ref,
                    w1_ref, w2_ref, gfin_ref, y_ref):
    mod = mod_ref[...]
    x1, hm = _mix_in(h_a, out_scale, x_ref[...], mod, oa_ref, ob_ref, gsub_ref, wout_ref)
    ff = _ffn_chunk(hm, w1_ref[0], w2_ref[0])
    for c in range(1, w1_ref.shape[0]):
        ff = ff + _ffn_chunk(hm, w1_ref[c], w2_ref[c])
    y_ref[...] = _mix_out(final, x1, mod, ff, gfin_ref)


def _mix_in(h_a, out_scale, x, mod, oa_ref, ob_ref, gsub_ref, wout_ref):
    g, r, d = x.shape
    gsub = gsub_ref[...] * out_scale
    parts = [(_rms(oa_ref[:, _slab(h)]) * gsub).astype(BF16) for h in range(h_a)]
    parts.append(ob_ref[...].astype(BF16))
    att = _dot(jnp.concatenate(parts, axis=1), wout_ref[...])
    x1 = x + mod[:, 2:3, :] * att.reshape(g, r, d)
    hm = (_rms(x1) * (1.0 + mod[:, 4:5, :]) + mod[:, 3:4, :]).reshape(g * r, d).astype(BF16)
    return x1, hm


def _ffn_chunk(hm, w1c, w2c):
    a = _dot(hm, w1c)
    return _dot(jnp.square(jnp.maximum(a, 0.0)).astype(BF16), w2c)


def _mix_out(final, x1, mod, ff, gfin_ref):
    x2 = x1 + mod[:, 5:6, :] * ff.reshape(x1.shape)
    if final:
        x2 = _rms(x2) * gfin_ref[...]
    return x2


def _mix_ffn(h_a, out_scale, final, x3, mod3, oa, ob, gsub, wout, w1, w2, gfin, g_blk, tiles_per_mod):
    ng, r, d = x3.shape
    tm = g_blk * r
    row = lambda a: pl.BlockSpec((tm, a.shape[1]), lambda i: (i, 0))
    return pl.pallas_call(
        functools.partial(_mix_ffn_kernel, h_a, out_scale, final),
        grid=(ng // g_blk,),
        in_specs=[pl.BlockSpec((g_blk, r, d), lambda i: (i, 0, 0)),
                  pl.BlockSpec((g_blk, 6, d), lambda i: (i // tiles_per_mod, 0, 0)),
                  row(oa), row(ob), _full(gsub), _full(wout), _full(w1), _full(w2), _full(gfin)],
        out_specs=pl.BlockSpec((g_blk, r, d), lambda i: (i, 0, 0)),
        out_shape=jax.ShapeDtypeStruct((ng, r, d), F32),
        compiler_params=_cparams(("parallel",)),
        name="mix_ffn",
    )(x3, mod3, oa, ob, gsub, wout, w1, w2, gfin)


def _rope_angles(pos, d):
    inv = 1.0 / (ROPE_THETA ** (jnp.arange(0, d, 2, dtype=F32) / d))
    ang = pos.astype(F32)[:, None] * inv[None, :]
    return jnp.cos(ang), jnp.sin(ang)


def _rope_tables(pos, d_a, d_b):
    (ca, sa), (cb, sb) = _rope_angles(pos, d_a), _rope_angles(pos, d_b)
    n = pos.shape[0]
    reps = LANES // d_a
    cosa = jnp.tile(jnp.concatenate([ca, ca], axis=1), (1, reps))
    sina = jnp.tile(jnp.concatenate([-sa, sa], axis=1), (1, reps))
    one, zero = jnp.ones((n, 64), F32), jnp.zeros((n, 64), F32)
    cosb = jnp.concatenate([one, cb, cb, one[:, :LANES - 64 - d_b]], axis=1)
    sinb = jnp.concatenate([zero, -sb, sb, zero[:, :LANES - 64 - d_b]], axis=1)
    return cosa, sina, cosb, sinb


def kernel(x_prompt, x_sample, c_prompt, c_sample, cache_diff_k, cache_diff_v, cache_mla_ckv, cache_mla_krope,
           page_table, w_ada, b_ada, w_in, g_q, w_uq, w_uk, g_kv, lambda_q1, lambda_k1, lambda_q2, lambda_k2,
           g_sub, w_uv, w_out, w_ff1, w_ff2, g_final):
    batch, seq, d = x_prompt.shape
    s_batch, s_seq, _ = x_sample.shape
    depth, n_phys, page, h_a, _, dk_a = cache_diff_k.shape
    dv_a = cache_diff_v.shape[-1]
    kv_lora = cache_mla_ckv.shape[-1]
    qk_rope = cache_mla_krope.shape[-1]
    h_b, _, qk_nope = w_uk.shape[1:]
    v_head = w_uv.shape[-1]
    q_lora = g_q.shape[-1]
    past_len = page_table.shape[1] * page
    n_a = h_a * 2 * dk_a
    assert (dk_a, dv_a, qk_nope, qk_rope, v_head) == (64, 128, 64, 32, 64), "head geometry the lane layouts assume"
    assert w_in.shape[-1] == 3 * n_a + q_lora + kv_lora + qk_rope
    cfg = (h_a, dk_a, h_b, q_lora, kv_lora, dk_a ** -0.5, (qk_nope + qk_rope) ** -0.5)

    tm_p, ts, ff_chunk = batch * seq // s_batch, 32, 1024
    assert tm_p * s_batch == batch * seq and seq % tm_p == 0 and tm_p % LANES == 0
    pos_p = jnp.arange(seq)
    (ca_p, sa_p), (cb_p, sb_p) = _rope_angles(pos_p, dk_a), _rope_angles(pos_p, qk_rope)
    tab_p = _rope_tables(pos_p, dk_a, qk_rope) + (ca_p.T, sa_p.T, cb_p.T, sb_p.T)
    tab_s = tuple(jnp.tile(a, (ts, 1)) for a in _rope_tables(past_len + jnp.arange(s_seq), dk_a, qk_rope))

    xp3 = x_prompt.reshape(batch * seq // tm_p, tm_p, d)
    xs3 = x_sample
    c_all = jnp.concatenate([c_prompt, c_sample], axis=0)
    ckt = jnp.transpose(cache_diff_k, (0, 1, 3, 4, 5, 2)).reshape(depth * n_phys, n_a, page)
    cv = cache_diff_v.reshape(depth * n_phys, page * h_a, dv_a)
    cc = cache_mla_ckv.reshape(depth * n_phys, page, kv_lora)
    crt = jnp.transpose(cache_mla_krope, (0, 1, 3, 2)).reshape(depth * n_phys, qk_rope, page)

    sel_np = np.zeros((h_b * LANES, h_b * qk_rope), np.float32)
    sel_rows = (np.arange(h_b)[:, None] * LANES + 64 + np.arange(qk_rope)[None, :]).reshape(-1)
    sel_np[sel_rows, np.arange(h_b * qk_rope)] = 1.0
    sel = jnp.asarray(sel_np, dtype=BF16)

    new_p, new_s = [], []
    for l in range(depth):
        lam_init = 0.8 - 0.6 * math.exp(-0.3 * l)
        mod = _adaln(c_all, w_ada[l], b_ada[l]).reshape(batch + s_batch, 6, d)
        mod_p, mod_s = mod[:batch], mod[batch:]
        off_ckv = 3 * n_a + q_lora
        off_kr = off_ckv + kv_lora
        w_l = w_in[l]
        win_s = jnp.concatenate([w_l[:, :off_kr], jnp.zeros((d, 64), F32), w_l[:, off_kr:],
                                 jnp.zeros((d, LANES - 64 - qk_rope), F32)], axis=1).astype(BF16)
        wmain = jnp.concatenate([w_l[:, :n_a], w_l[:, 2 * n_a:off_kr]], axis=1).astype(BF16)
        wkt = jnp.concatenate([w_l[:, n_a:2 * n_a], w_l[:, off_kr:]], axis=1).T.astype(BF16)
        wuq = jnp.pad(w_uq[l].reshape(q_lora, h_b, qk_nope + qk_rope),
                      ((0, 0), (0, 0), (0, LANES - qk_nope - qk_rope))).reshape(q_lora, h_b * LANES).astype(BF16)
        wukt = jnp.transpose(w_uk[l], (0, 2, 1)).reshape(h_b * qk_nope, kv_lora).astype(BF16)
        wuv_all = jnp.transpose(w_uv[l], (1, 0, 2)).reshape(kv_lora, h_b * v_head).astype(BF16)
        wabs = jnp.pad(jnp.transpose(w_uk[l], (0, 2, 1)), ((0, 0), (0, LANES - qk_nope), (0, 0))).astype(BF16)
        gq, gkv, gsub = g_q[l].reshape(1, -1), g_kv[l].reshape(1, -1), g_sub[l].reshape(1, -1)
        lamv = jnp.stack([lambda_q1[l], lambda_k1[l], lambda_q2[l], lambda_k2[l]], axis=0)
        d_ff = w_ff1.shape[-1]
        n_ch = d_ff // ff_chunk
        wout = w_out[l].astype(BF16)
        w1 = jnp.transpose(w_ff1[l].reshape(d, n_ch, ff_chunk), (1, 0, 2)).astype(BF16)
        w2 = w_ff2[l].reshape(n_ch, ff_chunk, d).astype(BF16)
        gfin = g_final.reshape(1, d)
        final = l == depth - 1

        q, kat, krt, kt, v, va, ckv = _proj_prompt(
            cfg, xp3, mod_p, tab_p, (wmain, wkt, gq, wuq, gkv, wukt, wuv_all), batch, seq)
        oa, ob = _prompt_attn(h_a, h_b, lam_init, batch, seq, tm_p, q, kt, v, lamv)
        new_p.append((jnp.transpose(kat.reshape(batch, h_a, 2, dk_a, seq), (0, 4, 1, 2, 3)),
                      va.reshape(batch, seq, h_a, dv_a),
                      ckv.reshape(batch, seq, kv_lora),
                      jnp.transpose(krt, (0, 2, 1))))

        sqa, ska, sva, sckv, skr, sqlat, sqrope = _proj_sample(
            cfg, xs3, mod_s, tab_s, (win_s, gq, wuq, gkv, wabs, sel), ts)
        r3 = lambda a: a.reshape(s_batch, s_seq, a.shape[-1])
        soa, sob, xp3 = _sample_attn(h_a, h_b, kv_lora, lam_init, final, l * n_phys, page_table, r3(sqa), r3(sqlat),
                                     r3(sqrope), r3(ska), r3(sva), r3(sckv), r3(skr), lamv, wuv_all, ckt, cv, cc, crt,
                                     xp3, mod_p, oa, ob, gsub, wout, w1, w2, gfin)
        n_s = s_batch * s_seq
        xs3 = _mix_ffn(h_a, 1.0 - lam_init, final, xs3, mod_s, soa.reshape(n_s, -1), sob.reshape(n_s, -1),
                       gsub, wout, w1, w2, gfin, ts, 1)
        new_s.append((ska.reshape(s_batch, s_seq, h_a, 2, dk_a), sva.reshape(s_batch, s_seq, h_a, dv_a),
                      sckv.reshape(s_batch, s_seq, kv_lora), skr.reshape(s_batch, s_seq, qk_rope)))

    y_prompt = xp3.reshape(batch, seq, d)
    y_sample = xs3
    stack = lambda rows, i: jnp.stack([r[i] for r in rows], axis=0)
    return (y_prompt, y_sample, stack(new_p, 0), stack(new_p, 1), stack(new_p, 2), stack(new_p, 3),
            stack(new_s, 0), stack(new_s, 1), stack(new_s, 2), stack(new_s, 3))
```

```python
import functools
import math

import jax
import jax.numpy as jnp
import numpy as np
from jax import lax
from jax.experimental import pallas as pl
from jax.experimental.pallas import tpu as pltpu

F32 = jnp.float32
BF16 = jnp.bfloat16

ROPE_THETA = 10000.0
EPS = 1e-6
NEG_INF = -1e30
LOG2E = math.log2(math.e)
LANES = 128
VMEM_LIMIT = 56 * 1024 * 1024


def _cparams(sem):
    return pltpu.CompilerParams(dimension_semantics=sem, vmem_limit_bytes=VMEM_LIMIT)


def _rms(x):
    return x * lax.rsqrt(jnp.mean(x * x, axis=-1, keepdims=True) + EPS)


def _dot(a, b):
    return jnp.dot(a, b, preferred_element_type=F32)


def _dot_nt(a, b):
    return lax.dot_general(a, b, (((1,), (1,)), ((), ())), preferred_element_type=F32)


def _slab(i, w=LANES):
    return slice(i * w, (i + 1) * w)


def _log2(n):
    assert n & (n - 1) == 0, "power of two expected"
    return n.bit_length() - 1


def _rope(x, cos, sin_signed, half):
    lane = lax.broadcasted_iota(jnp.int32, x.shape, 1)
    first = (lane & (2 * half - 1)) < half
    swapped = jnp.where(first, pltpu.roll(x, LANES - half, 1), pltpu.roll(x, half, 1))
    return x * cos + swapped * sin_signed


def _rope_rows(x1, x2, cos, sin):
    return x1 * cos - x2 * sin, x2 * cos + x1 * sin


def _lambda(lamv, lam_init):
    e1 = jnp.exp(jnp.sum(lamv[0:1] * lamv[1:2], axis=-1, keepdims=True))
    e2 = jnp.exp(jnp.sum(lamv[2:3] * lamv[3:4], axis=-1, keepdims=True))
    return e1 - e2 + lam_init


def _full(a):
    return pl.BlockSpec(a.shape, lambda *idx, nd=a.ndim: (0,) * nd)


def _adaln_kernel(c_ref, w_ref, b_ref, o_ref):
    c = c_ref[...]
    s = c / (1.0 + jnp.exp(-c))
    o_ref[...] = _dot(s.astype(BF16), w_ref[...].astype(BF16)) + b_ref[...]


def _adaln(c_all, w_ada, b_ada):
    n, d = c_all.shape
    d_out = w_ada.shape[1]
    tn = 1024
    return pl.pallas_call(
        _adaln_kernel,
        grid=(d_out // tn,),
        in_specs=[pl.BlockSpec((n, d), lambda j: (0, 0)),
                  pl.BlockSpec((d, tn), lambda j: (0, j)),
                  pl.BlockSpec((1, tn), lambda j: (0, j))],
        out_specs=pl.BlockSpec((n, tn), lambda j: (0, j)),
        out_shape=jax.ShapeDtypeStruct((n, d_out), F32),
        compiler_params=_cparams(("arbitrary",)),
        name="adaln",
    )(c_all, w_ada, b_ada.reshape(1, d_out))


def _modulated(x_ref, mod_ref):
    x = x_ref[...]
    g, r, d = x.shape
    mod = mod_ref[...]
    h = _rms(x) * (1.0 + mod[:, 1:2, :]) + mod[:, 0:1, :]
    return h.reshape(g * r, d).astype(BF16)


def _proj_prompt_kernel(cfg, x_ref, mod_ref, cosa_ref, sina_ref, cosb_ref, sinb_ref, cosat_ref, sinat_ref,
                        cosbt_ref, sinbt_ref, wmain_ref, wkt_ref, gq_ref, wuq_ref, gkv_ref, wukt_ref, wuv_ref,
                        q_ref, kat_ref, krt_ref, kt_ref, v_ref, va_ref, ckv_ref):
    h_a, dk_a, h_b, q_lora, kv_lora, scale_a, scale_b = cfg
    scale_a, scale_b = scale_a * LOG2E, scale_b * LOG2E
    n_a = h_a * 2 * dk_a
    hmod = _modulated(x_ref, mod_ref)
    tm = hmod.shape[0]
    p = _dot(hmod, wmain_ref[...])
    kk = _dot_nt(wkt_ref[...], hmod)
    cosa, sina, cosb, sinb = cosa_ref[...], sina_ref[...], cosb_ref[...], sinb_ref[...]
    lane = lax.broadcasted_iota(jnp.int32, (tm, LANES), 1)
    low = lane < dk_a

    for h in range(h_a):
        q = _rope(p[:, _slab(h)], cosa, sina, dk_a // 2) * scale_a
        q_ref[:, _slab(2 * h)] = jnp.where(low, q, 0.0).astype(BF16)
        q_ref[:, _slab(2 * h + 1)] = jnp.where(low, 0.0, q).astype(BF16)
    cq = _rms(p[:, 2 * n_a:2 * n_a + q_lora]) * gq_ref[...]
    qb = _dot(cq.astype(BF16), wuq_ref[...])
    for h in range(h_b):
        q_ref[:, _slab(2 * h_a + h)] = (_rope(qb[:, _slab(h)], cosb, sinb, 16) * scale_b).astype(BF16)

    ca, sa = cosat_ref[...], sinat_ref[...]
    hd = dk_a // 2
    for g in range(n_a // dk_a):
        o1, o2 = _rope_rows(kk[g * dk_a:g * dk_a + hd], kk[g * dk_a + hd:(g + 1) * dk_a], ca, sa)
        kat_ref[g * dk_a:g * dk_a + hd, :] = o1
        kat_ref[g * dk_a + hd:(g + 1) * dk_a, :] = o2
        kt_ref[g * dk_a:g * dk_a + hd, :] = o1.astype(BF16)
        kt_ref[g * dk_a + hd:(g + 1) * dk_a, :] = o2.astype(BF16)
    r1, r2 = _rope_rows(kk[n_a:n_a + 16], kk[n_a + 16:n_a + 32], cosbt_ref[...], sinbt_ref[...])
    krt_ref[0:16, :] = r1
    krt_ref[16:32, :] = r2

    v = p[:, n_a:2 * n_a]
    for h in range(h_a):
        va_ref[pl.ds(h, tm, stride=h_a), :] = v[:, _slab(h)]
    v_ref[:, 0:n_a] = v.astype(BF16)
    ckv = _rms(p[:, 2 * n_a + q_lora:]) * gkv_ref[...]
    ckv_ref[...] = ckv
    ckv_bf = ckv.astype(BF16)
    v_ref[:, n_a:] = _dot(ckv_bf, wuv_ref[...]).astype(BF16)

    knt = _dot_nt(wukt_ref[...], ckv_bf)
    r1b, r2b = r1.astype(BF16), r2.astype(BF16)
    zero = jnp.zeros((32, tm), BF16)
    for h in range(h_b):
        base = n_a + h * LANES
        kt_ref[base:base + 64, :] = knt[h * 64:(h + 1) * 64].astype(BF16)
        kt_ref[base + 64:base + 80, :] = r1b
        kt_ref[base + 80:base + 96, :] = r2b
        kt_ref[base + 96:base + 128, :] = zero


def _proj_prompt(cfg, x3, mod3, tables, weights, batch, seq):
    ng, tm, d = x3.shape
    n = ng * tm
    tpb = seq // tm
    h_a, dk_a, h_b, q_lora, kv_lora, _, _ = cfg
    n_a = h_a * 2 * dk_a
    tab = pl.BlockSpec((tm, LANES), lambda i: (i % tpb, 0))
    tabt = lambda a: pl.BlockSpec((a.shape[0], tm), lambda i: (0, i % tpb))
    in_specs = [pl.BlockSpec((1, tm, d), lambda i: (i, 0, 0)),
                pl.BlockSpec((1, 6, d), lambda i: (i // tpb, 0, 0)),
                tab, tab, tab, tab] + [tabt(a) for a in tables[4:]] + [_full(w) for w in weights]
    row = lambda w: pl.BlockSpec((tm, w), lambda i: (i, 0))
    nq = (2 * h_a + h_b) * LANES
    nk = n_a + h_b * LANES
    nv = n_a + h_b * 64
    out_specs = [row(nq),
                 pl.BlockSpec((None, n_a, tm), lambda i: (i // tpb, 0, i % tpb)),
                 pl.BlockSpec((None, 32, tm), lambda i: (i // tpb, 0, i % tpb)),
                 pl.BlockSpec((None, None, nk, tm), lambda i: (i // tpb, i % tpb, 0, 0)),
                 row(nv),
                 pl.BlockSpec((tm * h_a, LANES), lambda i: (i, 0)),
                 row(kv_lora)]
    out_shape = [jax.ShapeDtypeStruct((n, nq), BF16),
                 jax.ShapeDtypeStruct((batch, n_a, seq), F32),
                 jax.ShapeDtypeStruct((batch, 32, seq), F32),
                 jax.ShapeDtypeStruct((batch, tpb, nk, tm), BF16),
                 jax.ShapeDtypeStruct((n, nv), BF16),
                 jax.ShapeDtypeStruct((n * h_a, LANES), F32),
                 jax.ShapeDtypeStruct((n, kv_lora), F32)]
    return pl.pallas_call(
        functools.partial(_proj_prompt_kernel, cfg),
        grid=(ng,),
        in_specs=in_specs,
        out_specs=out_specs,
        out_shape=out_shape,
        compiler_params=_cparams(("parallel",)),
        name="proj_prompt",
    )(x3, mod3, *tables, *weights)


def _proj_sample_kernel(cfg, x_ref, mod_ref, cosa_ref, sina_ref, cosb_ref, sinb_ref, win_ref, gq_ref, wuq_ref,
                        gkv_ref, wabs_ref, sel_ref, qa_ref, ka_ref, va_ref, ckv_ref, kr_ref, qlat_ref, qrope_ref):
    h_a, dk_a, h_b, q_lora, kv_lora, scale_a, scale_b = cfg
    n_a = h_a * 2 * dk_a
    off_cq = 3 * n_a
    off_ckv = off_cq + q_lora
    off_kr = off_ckv + kv_lora
    hmod = _modulated(x_ref, mod_ref)
    p = _dot(hmod, win_ref[...])
    cosa, sina, cosb, sinb = cosa_ref[...], sina_ref[...], cosb_ref[...], sinb_ref[...]
    for s in range(n_a // LANES):
        qa_ref[:, _slab(s)] = _rope(p[:, _slab(s)], cosa, sina, dk_a // 2) * scale_a
        ka_ref[:, _slab(s)] = _rope(p[:, n_a + s * LANES:n_a + (s + 1) * LANES], cosa, sina, dk_a // 2)
    va_ref[...] = p[:, 2 * n_a:3 * n_a]
    cq = _rms(p[:, off_cq:off_ckv]) * gq_ref[...]
    qb = _dot(cq.astype(BF16), wuq_ref[...])
    ckv_ref[...] = _rms(p[:, off_ckv:off_kr]) * gkv_ref[...]
    kr_ref[...] = _rope(p[:, off_kr:off_kr + LANES], cosb, sinb, 16)[:, 64:96]
    qbr = [(_rope(qb[:, _slab(h)], cosb, sinb, 16) * scale_b).astype(BF16) for h in range(h_b)]
    for h in range(h_b):
        qlat_ref[:, h * kv_lora:(h + 1) * kv_lora] = _dot(qbr[h], wabs_ref[h])
    qrope_ref[...] = _dot(jnp.concatenate(qbr, axis=1), sel_ref[...])


def _proj_sample(cfg, x3, mod3, tables, weights, g_blk):
    ng, r, d = x3.shape
    tm = g_blk * r
    n = ng * r
    h_a, dk_a, h_b, q_lora, kv_lora, _, _ = cfg
    n_a = h_a * 2 * dk_a
    tab = pl.BlockSpec((tm, LANES), lambda i: (0, 0))
    in_specs = [pl.BlockSpec((g_blk, r, d), lambda i: (i, 0, 0)),
                pl.BlockSpec((g_blk, 6, d), lambda i: (i, 0, 0)),
                tab, tab, tab, tab] + [_full(w) for w in weights]
    widths = [n_a, n_a, n_a, kv_lora, 32, h_b * kv_lora, h_b * 32]
    return pl.pallas_call(
        functools.partial(_proj_sample_kernel, cfg),
        grid=(ng // g_blk,),
        in_specs=in_specs,
        out_specs=[pl.BlockSpec((tm, w), lambda i: (i, 0)) for w in widths],
        out_shape=[jax.ShapeDtypeStruct((n, w), F32) for w in widths],
        compiler_params=_cparams(("parallel",)),
        name="proj_sample",
    )(x3, mod3, *tables, *weights)


def _prompt_attn_kernel(h_a, h_b, lam_init, tq, q_ref, kt_ref, v_ref, lamv_ref, oa_ref, ob_ref,
                        m_ref, l_ref, acc_ref):
    qi = pl.program_id(1)
    n_streams = 2 * h_a + h_b
    row = lax.broadcasted_iota(jnp.int32, (tq, tq), 0)
    col = lax.broadcasted_iota(jnp.int32, (tq, tq), 1)
    causal = col <= row

    def slabs(st):
        if st < 2 * h_a:
            return st // 2, st // 2
        h = st - 2 * h_a
        return h_a + h, h_a + h // 2

    m_ref[...] = jnp.full(m_ref.shape, NEG_INF, F32)
    l_ref[...] = jnp.zeros(l_ref.shape, F32)
    acc_ref[...] = jnp.zeros(acc_ref.shape, F32)

    def block(js, masked):
        for st in range(n_streams):
            ks, vs = slabs(st)
            q = q_ref[:, _slab(st)]
            ss = [_dot(q, kt_ref[j, _slab(ks), :]) for j in js]
            if masked:
                ss[-1] = jnp.where(causal, ss[-1], NEG_INF)
            s = jnp.concatenate(ss, axis=1)
            n_tiles = s.shape[1] // LANES
            m_old = m_ref[st]
            m_new = jnp.maximum(m_old, jnp.max(s, axis=-1, keepdims=True))
            alpha = jnp.exp2(m_old - m_new)
            p = jnp.exp2(s - jnp.concatenate([m_new] * n_tiles, axis=1))
            psum = p[:, 0:LANES]
            for c in range(1, n_tiles):
                psum = psum + p[:, _slab(c)]
            l_ref[st] = alpha * l_ref[st] + psum
            m_ref[st] = m_new
            pv = None
            for i, j in enumerate(js):
                k0 = pl.multiple_of(j * tq, tq)
                d = _dot(p[:, i * tq:(i + 1) * tq].astype(BF16), v_ref[pl.ds(k0, tq), _slab(vs)])
                pv = d if pv is None else pv + d
            acc_ref[st] = alpha * acc_ref[st] + pv

    def body(jj, carry):
        block([2 * jj], False)
        block([2 * jj + 1], False)
        return carry
    lax.fori_loop(0, qi // 2, body, 0)

    @pl.when(qi % 2 == 1)
    def _():
        block([qi - 1], False)
    block([qi], True)

    lam = _lambda(lamv_ref[...], lam_init)
    out = lambda st: acc_ref[st] / jnp.sum(l_ref[st], axis=-1, keepdims=True)
    for h in range(h_a):
        oa_ref[:, _slab(h)] = out(2 * h) - lam * out(2 * h + 1)
    low = lax.broadcasted_iota(jnp.int32, (tq, LANES), 1) < 64
    for hp in range(h_b // 2):
        ob_ref[:, _slab(hp)] = jnp.where(low, out(2 * h_a + 2 * hp), out(2 * h_a + 2 * hp + 1)).astype(BF16)


def _prompt_attn(h_a, h_b, lam_init, batch, seq, tq, q, kt, v, lamv):
    nq = seq // tq
    n = batch * seq
    n_streams = 2 * h_a + h_b
    qspec = lambda w: pl.BlockSpec((tq, w), lambda b, i: (b * nq + i, 0))
    return pl.pallas_call(
        functools.partial(_prompt_attn_kernel, h_a, h_b, lam_init, tq),
        grid=(batch, nq),
        in_specs=[qspec(q.shape[1]),
                  pl.BlockSpec((None,) + kt.shape[1:], lambda b, i: (b, 0, 0, 0)),
                  pl.BlockSpec((seq, v.shape[1]), lambda b, i: (b, 0)),
                  _full(lamv)],
        out_specs=[qspec(h_a * LANES), qspec(h_b * 64)],
        out_shape=[jax.ShapeDtypeStruct((n, h_a * LANES), F32), jax.ShapeDtypeStruct((n, h_b * 64), BF16)],
        scratch_shapes=[pltpu.VMEM((n_streams, tq, LANES), F32)] * 3,
        compiler_params=_cparams(("parallel", "arbitrary")),
        name="prompt_attn",
    )(q, kt, v, lamv)


def _sample_attn_kernel(h_a, h_b, kv_lora, lam_init, final, npg, base, pt_ref, qa_ref, qlat_ref, qrope_ref, kn_ref,
                        vn_ref, cn_ref, rn_ref, lamv_ref, wuv_ref,
                        x_ref, mod_ref, poa_ref, pob_ref, gsub_ref, wout_ref, w1_ref, w2_ref, gfin_ref,
                        ck_hbm, cv_hbm, cc_hbm, cr_hbm, oa_ref, ob_ref, y_ref,
                        qbd, ql, qr, m_a, l_a, acc_a, m_b, l_b, acc_b, kbuf, vbuf, cbuf, rbuf, sem,
                        x1_scr, hm_scr, ff_scr):
    i, j = pl.program_id(0), pl.program_id(1)
    steps = pl.num_programs(1)
    total = pl.num_programs(0) * steps
    n = i * steps + j
    slot = n % 2
    t = qa_ref.shape[0]
    page = cbuf.shape[2]
    rows_a = 2 * h_a * t
    rows_b = h_b * t
    caches, bufs = (ck_hbm, cv_hbm, cc_hbm, cr_hbm), (kbuf, vbuf, cbuf, rbuf)

    def page_copies(step, dst_slot):
        si, sj = step // steps, step % steps
        copies = []
        for k in range(npg):
            pid = base + pt_ref[si, sj * npg + k]
            for kind in range(4):
                copies.append(pltpu.make_async_copy(caches[kind].at[pid], bufs[kind].at[dst_slot, k],
                                                    sem.at[dst_slot, kind]))
        return copies

    def fetch(step, dst_slot):
        for c in page_copies(step, dst_slot):
            c.start()

    @pl.when(n == 0)
    def _():
        fetch(0, 0)

        @pl.when(total > 1)
        def _():
            fetch(1, 1)

    def softmax_step(m_ref, l_ref, s):
        m_old = m_ref[...]
        m_new = jnp.maximum(m_old, jnp.max(s, axis=-1, keepdims=True))
        alpha = jnp.exp(m_old - m_new)
        p = jnp.exp(s - m_new)
        l_ref[...] = alpha * l_ref[...] + jnp.sum(p, axis=-1, keepdims=True)
        m_ref[...] = m_new
        return alpha, p

    def attend(sa, sb, n_chunks, val_a, val_b):
        alpha, p = softmax_step(m_a, l_a, sa)
        pv = []
        for h in range(h_a):
            ph = p[2 * t * h:2 * t * (h + 1)]
            acc = _dot(ph[:, 0:page].astype(BF16), val_a(h, 0))
            for i in range(1, n_chunks):
                acc = acc + _dot(ph[:, i * page:(i + 1) * page].astype(BF16), val_a(h, i))
            pv.append(acc)
        acc_a[...] = alpha * acc_a[...] + jnp.concatenate(pv, axis=0)
        alpha, p = softmax_step(m_b, l_b, sb)
        acc = _dot(p[:, 0:page].astype(BF16), val_b(0))
        for i in range(1, n_chunks):
            acc = acc + _dot(p[:, i * page:(i + 1) * page].astype(BF16), val_b(i))
        acc_b[...] = alpha * acc_b[...] + acc

    @pl.when(j == 0)
    def _():
        qa = qa_ref[...]
        r = lax.broadcasted_iota(jnp.int32, (rows_a, qa.shape[1]), 0)
        c = lax.broadcasted_iota(jnp.int32, (rows_a, qa.shape[1]), 1)
        dk = qa.shape[1] // (2 * h_a)
        qbd[...] = jnp.where((r >> _log2(t)) == (c >> _log2(dk)),
                             jnp.concatenate([qa] * (2 * h_a), axis=0), 0.0).astype(BF16)
        ql[...] = jnp.concatenate([qlat_ref[:, h * kv_lora:(h + 1) * kv_lora] for h in range(h_b)],
                                  axis=0).astype(BF16)
        qr[...] = jnp.concatenate([qrope_ref[:, h * 32:(h + 1) * 32] for h in range(h_b)], axis=0).astype(BF16)
        for m_ref, l_ref, acc_ref in ((m_a, l_a, acc_a), (m_b, l_b, acc_b)):
            m_ref[...] = jnp.full(m_ref.shape, NEG_INF, F32)
            l_ref[...] = jnp.zeros(l_ref.shape, F32)
            acc_ref[...] = jnp.zeros(acc_ref.shape, F32)
        pad = lambda ref: jnp.concatenate(
            [ref[...], jnp.zeros((page - t, ref.shape[1]), F32)], axis=0).astype(BF16)
        kn, vn, cn, rn = pad(kn_ref), pad(vn_ref), pad(cn_ref), pad(rn_ref)
        ca = lax.broadcasted_iota(jnp.int32, (rows_a, page), 1)
        ra = lax.broadcasted_iota(jnp.int32, (rows_a, page), 0)
        cb = lax.broadcasted_iota(jnp.int32, (rows_b, page), 1)
        rb = lax.broadcasted_iota(jnp.int32, (rows_b, page), 0)
        sa = jnp.where(ca <= (ra & (t - 1)), _dot_nt(qbd[...], kn), NEG_INF)
        sb = jnp.where(cb <= (rb & (t - 1)), _dot_nt(ql[...], cn) + _dot_nt(qr[...], rn), NEG_INF)
        attend(sa, sb, 1, lambda h, i: vn[:, _slab(h)], lambda i: cn)

    for c in page_copies(n, slot):
        c.wait()
    cs = [cbuf[slot, k].astype(BF16) for k in range(npg)]
    sa = jnp.concatenate([_dot(qbd[...], kbuf[slot, k].astype(BF16)) for k in range(npg)], axis=1)
    sb = jnp.concatenate([_dot_nt(ql[...], cs[k]) + _dot(qr[...], rbuf[slot, k].astype(BF16))
                          for k in range(npg)], axis=1)
    attend(sa, sb, npg, lambda h, k: vbuf[slot, k, pl.ds(h, page, stride=h_a), :].astype(BF16), lambda k: cs[k])

    @pl.when(n + 2 < total)
    def _():
        fetch(n + 2, slot)

    @pl.when(j == pl.num_programs(1) - 1)
    def _():
        lam = _lambda(lamv_ref[...], lam_init)
        oa = acc_a[...] / l_a[...]
        for h in range(h_a):
            oa_ref[:, _slab(h)] = oa[(2 * h) * t:(2 * h + 1) * t] - lam * oa[(2 * h + 1) * t:(2 * h + 2) * t]
        olat = (acc_b[...] / l_b[...]).astype(BF16)
        ofull = _dot(olat, wuv_ref[...])
        cc = lax.broadcasted_iota(jnp.int32, (t, ofull.shape[1]), 1) >> 6
        ob = jnp.zeros((t, ofull.shape[1]), F32)
        for h in range(h_b):
            ob = jnp.where(cc == h, ofull[h * t:(h + 1) * t, :], ob)
        ob_ref[...] = ob

    mod = mod_ref[...]

    @pl.when(j == 0)
    def _():
        x1, hm = _mix_in(h_a, 1.0 - lam_init, x_ref[...], mod, poa_ref, pob_ref, gsub_ref, wout_ref)
        x1_scr[...] = x1
        hm_scr[...] = hm
        ff_scr[...] = jnp.zeros(ff_scr.shape, F32)
    ff_scr[...] += _ffn_chunk(hm_scr[...], w1_ref[j], w2_ref[j])

    @pl.when(j == steps - 1)
    def _():
        y_ref[...] = _mix_out(final, x1_scr[...], mod, ff_scr[...], gfin_ref)


def _sample_attn(h_a, h_b, kv_lora, lam_init, final, base, page_table, qa3, qlat3, qrope3, kn3, vn3, cn3, rn3,
                 lamv, wuv, cache_kt, cache_v, cache_c, cache_rt, xp3, mod_p, poa, pob, gsub, wout, w1, w2, gfin):
    s, t, n_a = qa3.shape
    n_pages = page_table.shape[1]
    steps = w1.shape[0]
    npg = n_pages // steps
    n_tiles, tm, d = xp3.shape
    assert n_tiles == s and n_pages % steps == 0
    tiles_per_mod = n_tiles // mod_p.shape[0]
    seq = lambda a: pl.BlockSpec((None,) + a.shape[1:], lambda i, j, pt: (i, 0, 0))
    once = lambda a: pl.BlockSpec(a.shape, lambda i, j, pt, nd=a.ndim: (0,) * nd, pipeline_mode=pl.Buffered(1))
    caches = (cache_kt, cache_v, cache_c, cache_rt)
    in_specs = [seq(qa3), seq(qlat3), seq(qrope3), seq(kn3), seq(vn3), seq(cn3), seq(rn3), _full(lamv), _full(wuv)]
    in_specs += [pl.BlockSpec((1, tm, d), lambda i, j, pt: (i, 0, 0)),
                 pl.BlockSpec((1, 6, d), lambda i, j, pt: (i // tiles_per_mod, 0, 0)),
                 pl.BlockSpec((tm, poa.shape[1]), lambda i, j, pt: (i, 0)),
                 pl.BlockSpec((tm, pob.shape[1]), lambda i, j, pt: (i, 0)),
                 _full(gsub), once(wout), once(w1), once(w2), _full(gfin)]
    in_specs += [pl.BlockSpec(memory_space=pl.ANY)] * len(caches)
    rows_a, rows_b = 2 * h_a * t, h_b * t
    dv_a = cache_v.shape[2]
    scratch = [pltpu.VMEM((rows_a, n_a), BF16), pltpu.VMEM((rows_b, kv_lora), BF16), pltpu.VMEM((rows_b, 32), BF16),
               pltpu.VMEM((rows_a, 1), F32), pltpu.VMEM((rows_a, 1), F32), pltpu.VMEM((rows_a, dv_a), F32),
               pltpu.VMEM((rows_b, 1), F32), pltpu.VMEM((rows_b, 1), F32), pltpu.VMEM((rows_b, kv_lora), F32)]
    scratch += [pltpu.VMEM((2, npg) + a.shape[1:], F32) for a in caches]
    scratch += [pltpu.SemaphoreType.DMA((2, len(caches)))]
    scratch += [pltpu.VMEM((1, tm, d), F32), pltpu.VMEM((tm, d), BF16), pltpu.VMEM((tm, d), F32)]
    out_wa, out_wb = h_a * dv_a, wuv.shape[1]
    grid_spec = pltpu.PrefetchScalarGridSpec(
        num_scalar_prefetch=1,
        grid=(s, steps),
        in_specs=in_specs,
        out_specs=[pl.BlockSpec((None, t, out_wa), lambda i, j, pt: (i, 0, 0)),
                   pl.BlockSpec((None, t, out_wb), lambda i, j, pt: (i, 0, 0)),
                   pl.BlockSpec((1, tm, d), lambda i, j, pt: (i, 0, 0))],
        scratch_shapes=scratch)
    return pl.pallas_call(
        functools.partial(_sample_attn_kernel, h_a, h_b, kv_lora, lam_init, final, npg, base),
        grid_spec=grid_spec,
        out_shape=[jax.ShapeDtypeStruct((s, t, out_wa), F32), jax.ShapeDtypeStruct((s, t, out_wb), F32),
                   jax.ShapeDtypeStruct(xp3.shape, F32)],
        compiler_params=_cparams(("arbitrary", "arbitrary")),
        name="sample_attn",
    )(page_table, qa3, qlat3, qrope3, kn3, vn3, cn3, rn3, lamv, wuv, xp3, mod_p, poa, pob, gsub, wout, w1, w2,
      gfin, *caches)


def _mix_ffn_kernel(h_a, out_scale, final, x_ref, mod_ref, oa_ref, ob_ref, gsub_ref, wout_ref,
                    w1_ref, w2_ref, gfin_ref, y_ref):
    mod = mod_ref[...]
    x1, hm = _mix_in(h_a, out_scale, x_ref[...], mod, oa_ref, ob_ref, gsub_ref, wout_ref)
    ff = _ffn_chunk(hm, w1_ref[0], w2_ref[0])
    for c in range(1, w1_ref.shape[0]):
        ff = ff + _ffn_chunk(hm, w1_ref[c], w2_ref[c])
    y_ref[...] = _mix_out(final, x1, mod, ff, gfin_ref)


def _mix_in(h_a, out_scale, x, mod, oa_ref, ob_ref, gsub_ref, wout_ref):
    g, r, d = x.shape
    gsub = gsub_ref[...] * out_scale
    parts = [(_rms(oa_ref[:, _slab(h)]) * gsub).astype(BF16) for h in range(h_a)]
    parts.append(ob_ref[...].astype(BF16))
    att = _dot(jnp.concatenate(parts, axis=1), wout_ref[...])
    x1 = x + mod[:, 2:3, :] * att.reshape(g, r, d)
    hm = (_rms(x1) * (1.0 + mod[:, 4:5, :]) + mod[:, 3:4, :]).reshape(g * r, d).astype(BF16)
    return x1, hm


def _ffn_chunk(hm, w1c, w2c):
    a = _dot(hm, w1c)
    return _dot(jnp.square(jnp.maximum(a, 0.0)).astype(BF16), w2c)


def _mix_out(final, x1, mod, ff, gfin_ref):
    x2 = x1 + mod[:, 5:6, :] * ff.reshape(x1.shape)
    if final:
        x2 = _rms(x2) * gfin_ref[...]
    return x2


def _mix_ffn(h_a, out_scale, final, x3, mod3, oa, ob, gsub, wout, w1, w2, gfin, g_blk, tiles_per_mod):
    ng, r, d = x3.shape
    tm = g_blk * r
    row = lambda a: pl.BlockSpec((tm, a.shape[1]), lambda i: (i, 0))
    return pl.pallas_call(
        functools.partial(_mix_ffn_kernel, h_a, out_scale, final),
        grid=(ng // g_blk,),
        in_specs=[pl.BlockSpec((g_blk, r, d), lambda i: (i, 0, 0)),
                  pl.BlockSpec((g_blk, 6, d), lambda i: (i // tiles_per_mod, 0, 0)),
                  row(oa), row(ob), _full(gsub), _full(wout), _full(w1), _full(w2), _full(gfin)],
        out_specs=pl.BlockSpec((g_blk, r, d), lambda i: (i, 0, 0)),
        out_shape=jax.ShapeDtypeStruct((ng, r, d), F32),
        compiler_params=_cparams(("parallel",)),
        name="mix_ffn",
    )(x3, mod3, oa, ob, gsub, wout, w1, w2, gfin)


def _rope_angles(pos, d):
    inv = 1.0 / (ROPE_THETA ** (jnp.arange(0, d, 2, dtype=F32) / d))
    ang = pos.astype(F32)[:, None] * inv[None, :]
    return jnp.cos(ang), jnp.sin(ang)


def _rope_tables(pos, d_a, d_b):
    (ca, sa), (cb, sb) = _rope_angles(pos, d_a), _rope_angles(pos, d_b)
    n = pos.shape[0]
    reps = LANES // d_a
    cosa = jnp.tile(jnp.concatenate([ca, ca], axis=1), (1, reps))
    sina = jnp.tile(jnp.concatenate([-sa, sa], axis=1), (1, reps))
    one, zero = jnp.ones((n, 64), F32), jnp.zeros((n, 64), F32)
    cosb = jnp.concatenate([one, cb, cb, one[:, :LANES - 64 - d_b]], axis=1)
    sinb = jnp.concatenate([zero, -sb, sb, zero[:, :LANES - 64 - d_b]], axis=1)
    return cosa, sina, cosb, sinb


def kernel(x_prompt, x_sample, c_prompt, c_sample, cache_diff_k, cache_diff_v, cache_mla_ckv, cache_mla_krope,
           page_table, w_ada, b_ada, w_in, g_q, w_uq, w_uk, g_kv, lambda_q1, lambda_k1, lambda_q2, lambda_k2,
           g_sub, w_uv, w_out, w_ff1, w_ff2, g_final):
    batch, seq, d = x_prompt.shape
    s_batch, s_seq, _ = x_sample.shape
    depth, n_phys, page, h_a, _, dk_a = cache_diff_k.shape
    dv_a = cache_diff_v.shape[-1]
    kv_lora = cache_mla_ckv.shape[-1]
    qk_rope = cache_mla_krope.shape[-1]
    h_b, _, qk_nope = w_uk.shape[1:]
    v_head = w_uv.shape[-1]
    q_lora = g_q.shape[-1]
    past_len = page_table.shape[1] * page
    n_a = h_a * 2 * dk_a
    assert (dk_a, dv_a, qk_nope, qk_rope, v_head) == (64, 128, 64, 32, 64), "head geometry the lane layouts assume"
    assert w_in.shape[-1] == 3 * n_a + q_lora + kv_lora + qk_rope
    cfg = (h_a, dk_a, h_b, q_lora, kv_lora, dk_a ** -0.5, (qk_nope + qk_rope) ** -0.5)

    tm_p, ts, ff_chunk = batch * seq // s_batch, 32, 1024
    assert tm_p * s_batch == batch * seq and seq % tm_p == 0 and tm_p % LANES == 0
    pos_p = jnp.arange(seq)
    (ca_p, sa_p), (cb_p, sb_p) = _rope_angles(pos_p, dk_a), _rope_angles(pos_p, qk_rope)
    tab_p = _rope_tables(pos_p, dk_a, qk_rope) + (ca_p.T, sa_p.T, cb_p.T, sb_p.T)
    tab_s = tuple(jnp.tile(a, (ts, 1)) for a in _rope_tables(past_len + jnp.arange(s_seq), dk_a, qk_rope))

    xp3 = x_prompt.reshape(batch * seq // tm_p, tm_p, d)
    xs3 = x_sample
    c_all = jnp.concatenate([c_prompt, c_sample], axis=0)
    ckt = jnp.transpose(cache_diff_k, (0, 1, 3, 4, 5, 2)).reshape(depth * n_phys, n_a, page)
    cv = cache_diff_v.reshape(depth * n_phys, page * h_a, dv_a)
    cc = cache_mla_ckv.reshape(depth * n_phys, page, kv_lora)
    crt = jnp.transpose(cache_mla_krope, (0, 1, 3, 2)).reshape(depth * n_phys, qk_rope, page)

    sel_np = np.zeros((h_b * LANES, h_b * qk_rope), np.float32)
    sel_rows = (np.arange(h_b)[:, None] * LANES + 64 + np.arange(qk_rope)[None, :]).reshape(-1)
    sel_np[sel_rows, np.arange(h_b * qk_rope)] = 1.0
    sel = jnp.asarray(sel_np, dtype=BF16)

    new_p, new_s = [], []
    for l in range(depth):
        lam_init = 0.8 - 0.6 * math.exp(-0.3 * l)
        mod = _adaln(c_all, w_ada[l], b_ada[l]).reshape(batch + s_batch, 6, d)
        mod_p, mod_s = mod[:batch], mod[batch:]
        off_ckv = 3 * n_a + q_lora
        off_kr = off_ckv + kv_lora
        w_l = w_in[l]
        win_s = jnp.concatenate([w_l[:, :off_kr], jnp.zeros((d, 64), F32), w_l[:, off_kr:],
                                 jnp.zeros((d, LANES - 64 - qk_rope), F32)], axis=1).astype(BF16)
        wmain = jnp.concatenate([w_l[:, :n_a], w_l[:, 2 * n_a:off_kr]], axis=1).astype(BF16)
        wkt = jnp.concatenate([w_l[:, n_a:2 * n_a], w_l[:, off_kr:]], axis=1).T.astype(BF16)
        wuq = jnp.pad(w_uq[l].reshape(q_lora, h_b, qk_nope + qk_rope),
                      ((0, 0), (0, 0), (0, LANES - qk_nope - qk_rope))).reshape(q_lora, h_b * LANES).astype(BF16)
        wukt = jnp.transpose(w_uk[l], (0, 2, 1)).reshape(h_b * qk_nope, kv_lora).astype(BF16)
        wuv_all = jnp.transpose(w_uv[l], (1, 0, 2)).reshape(kv_lora, h_b * v_head).astype(BF16)
        wabs = jnp.pad(jnp.transpose(w_uk[l], (0, 2, 1)), ((0, 0), (0, LANES - qk_nope), (0, 0))).astype(BF16)
        gq, gkv, gsub = g_q[l].reshape(1, -1), g_kv[l].reshape(1, -1), g_sub[l].reshape(1, -1)
        lamv = jnp.stack([lambda_q1[l], lambda_k1[l], lambda_q2[l], lambda_k2[l]], axis=0)
        d_ff = w_ff1.shape[-1]
        n_ch = d_ff // ff_chunk
        wout = w_out[l].astype(BF16)
        w1 = jnp.transpose(w_ff1[l].reshape(d, n_ch, ff_chunk), (1, 0, 2)).astype(BF16)
        w2 = w_ff2[l].reshape(n_ch, ff_chunk, d).astype(BF16)
        gfin = g_final.reshape(1, d)
        final = l == depth - 1

        q, kat, krt, kt, v, va, ckv = _proj_prompt(
            cfg, xp3, mod_p, tab_p, (wmain, wkt, gq, wuq, gkv, wukt, wuv_all), batch, seq)
        oa, ob = _prompt_attn(h_a, h_b, lam_init, batch, seq, tm_p, q, kt, v, lamv)
        new_p.append((jnp.transpose(kat.reshape(batch, h_a, 2, dk_a, seq), (0, 4, 1, 2, 3)),
                      va.reshape(batch, seq, h_a, dv_a),
                      ckv.reshape(batch, seq, kv_lora),
                      jnp.transpose(krt, (0, 2, 1))))

        sqa, ska, sva, sckv, skr, sqlat, sqrope = _proj_sample(
            cfg, xs3, mod_s, tab_s, (win_s, gq, wuq, gkv, wabs, sel), ts)
        r3 = lambda a: a.reshape(s_batch, s_seq, a.shape[-1])
        soa, sob, xp3 = _sample_attn(h_a, h_b, kv_lora, lam_init, final, l * n_phys, page_table, r3(sqa), r3(sqlat),
                                     r3(sqrope), r3(ska), r3(sva), r3(sckv), r3(skr), lamv, wuv_all, ckt, cv, cc, crt,
                                     xp3, mod_p, oa, ob, gsub, wout, w1, w2, gfin)
        n_s = s_batch * s_seq
        xs3 = _mix_ffn(h_a, 1.0 - lam_init, final, xs3, mod_s, soa.reshape(n_s, -1), sob.reshape(n_s, -1),
                       gsub, wout, w1, w2, gfin, ts, 1)
        new_s.append((ska.reshape(s_batch, s_seq, h_a, 2, dk_a), sva.reshape(s_batch, s_seq, h_a, dv_a),
                      sckv.reshape(s_batch, s_seq, kv_lora), skr.reshape(s_batch, s_seq, qk_rope)))

    y_prompt = xp3.reshape(batch, seq, d)
    y_sample = xs3
    stack = lambda rows, i: jnp.stack([r[i] for r in rows], axis=0)
    return (y_prompt, y_sample, stack(new_p, 0), stack(new_p, 1), stack(new_p, 2), stack(new_p, 3),
            stack(new_s, 0), stack(new_s, 1), stack(new_s, 2), stack(new_s, 3))
```
